```python
import math
import jax
import jax.numpy as jnp
from jax import lax
import numpy as np

D_MODEL = 1024
BATCH = 8
SEQ = 4096
DEPTH = 2

CHUNK = 64
Q_BLOCK = 128
D_MIX = D_MODEL
D_FF = 2816
CONV_K = 4
EPS = 1e-6
ROPE_THETA = 10000.0

GDN_HEADS = 4
GDN_DK = 64
GDN_DV = 64
GDN_QK_W = GDN_HEADS * GDN_DK
GDN_W = GDN_HEADS * GDN_DV

DIFF_HEADS = 4
DIFF_DH = 32
DIFF_DV = 2 * DIFF_DH
DIFF_QK_W = 2 * DIFF_HEADS * DIFF_DH
DIFF_W = DIFF_HEADS * DIFF_DV

SSD_HEADS = 4
SSD_P = 64
SSD_N = 128
SSD_G = 2
SSD_W = SSD_HEADS * SSD_P

MLSTM_HEADS = 4
MLSTM_DH = 64
MLSTM_W = MLSTM_HEADS * MLSTM_DH

IN_SPLITS = (GDN_QK_W, GDN_QK_W, GDN_W, GDN_W, GDN_HEADS, GDN_HEADS,
             DIFF_QK_W, DIFF_QK_W, DIFF_W,
             SSD_W, SSD_W + 2 * SSD_G * SSD_N, SSD_HEADS,
             MLSTM_W, MLSTM_W, MLSTM_W, MLSTM_W, MLSTM_HEADS, MLSTM_HEADS)
IN_COLS = sum(IN_SPLITS)

kernel_name = 'hybrid_parallel_head_group_encoder'


def rmsnorm(x, g):
    xf = x.astype(jnp.float32)
    y = xf * lax.rsqrt(jnp.mean(xf * xf, axis=-1, keepdims=True) + EPS)
    return (y * g.astype(jnp.float32)).astype(x.dtype)


def l2norm(t):
    return t * lax.rsqrt(jnp.sum(t * t, axis=-1, keepdims=True) + EPS)


def swiglu(h, w_gate, w_up, w_down):
    return (jax.nn.silu(h @ w_gate) * (h @ w_up)) @ w_down


def causal_dwconv(u, w):
    ch = u.shape[-1]
    return lax.conv_general_dilated(u, w.astype(u.dtype)[:, None, :], window_strides=(1,),
                                    padding=[(CONV_K - 1, 0)],
                                    dimension_numbers=('NWC', 'WIO', 'NWC'),
                                    feature_group_count=ch)


def causal_tril():
    return jnp.tril(jnp.ones((CHUNK, CHUNK), dtype=bool))


def masked_decay(cum):
    diff = cum[..., :, None] - cum[..., None, :]
    return jnp.exp(jnp.where(causal_tril(), diff, -jnp.inf))


def heads_to_chunks(t):
    b, s = t.shape[:2]
    t = t.reshape((b, s // CHUNK, CHUNK) + t.shape[2:])
    perm = (1, 0, 3, 2) + tuple(range(4, t.ndim))
    return t.transpose(perm)


def chunks_to_heads(t):
    n, b, h, c, d = t.shape
    return t.transpose(1, 0, 3, 2, 4).reshape(b, n * c, h, d)


def rope_tables(seq, dim):
    inv_freq = ROPE_THETA ** (-jnp.arange(0, dim, 2, dtype=jnp.float32) / dim)
    ang = jnp.arange(seq, dtype=jnp.float32)[:, None] * inv_freq[None, :]
    return jnp.cos(ang), jnp.sin(ang)


def apply_rope(t, cos, sin):
    half = t.shape[-1] // 2
    t1, t2 = t[..., :half], t[..., half:]
    c, s = cos[None, :, None, :], sin[None, :, None, :]
    return jnp.concatenate([t1 * c - t2 * s, t2 * c + t1 * s], axis=-1)


def gated_deltanet(q, k, v, gate, beta_raw, a_raw, conv_w, a_log, dt_bias, norm_g):
    bsz, seq, _ = q.shape
    qkv = jax.nn.silu(causal_dwconv(jnp.concatenate([q, k, v], axis=-1), conv_w))
    q, k, v = jnp.split(qkv, [GDN_QK_W, 2 * GDN_QK_W], axis=-1)
    q = l2norm(q.reshape(bsz, seq, GDN_HEADS, GDN_DK)) * (GDN_DK ** -0.5)
    k = l2norm(k.reshape(bsz, seq, GDN_HEADS, GDN_DK))
    v = v.reshape(bsz, seq, GDN_HEADS, GDN_DV)
    beta = jax.nn.sigmoid(beta_raw)
    g = -jnp.exp(a_log) * jax.nn.softplus(a_raw + dt_bias)
    qc, kc, vc = heads_to_chunks(q), heads_to_chunks(k), heads_to_chunks(v)
    gc = jnp.cumsum(heads_to_chunks(g), axis=-1)
    bc = heads_to_chunks(beta)[..., None]
    decay = masked_decay(gc)
    eye = jnp.eye(CHUNK, dtype=jnp.float32)
    kb = kc * bc
    lower = jnp.einsum('nbhcd,nbhmd->nbhcm', kb, kc) * decay * (1.0 - eye)
    t_mat = lax.linalg.triangular_solve(eye + lower, jnp.broadcast_to(eye, lower.shape),
                                        left_side=True, lower=True, unit_diagonal=True)
    u = t_mat @ (vc * bc)
    w = t_mat @ (kb * jnp.exp(gc)[..., None])
    attn = jnp.einsum('nbhcd,nbhmd->nbhcm', qc, kc) * decay
    q_dec = qc * jnp.exp(gc)[..., None]
    k_dec = kc * jnp.exp(gc[..., -1:] - gc)[..., None]
    g_last = jnp.exp(gc[..., -1])

    def step(state, inp):
        q_i, k_i, u_i, w_i, a_i, gl_i = inp
        v_new = u_i - w_i @ state
        o_i = q_i @ state + a_i @ v_new
        state = state * gl_i[..., None, None] + jnp.einsum('bhcd,bhce->bhde', k_i, v_new)
        return state, o_i

    s0 = jnp.zeros((bsz, GDN_HEADS, GDN_DK, GDN_DV), jnp.float32)
    _, o = lax.scan(step, s0, (q_dec, k_dec, u, w, attn, g_last))
    o = chunks_to_heads(o)
    o = rmsnorm(o, norm_g) * jax.nn.silu(gate.reshape(bsz, seq, GDN_HEADS, GDN_DV))
    return o.reshape(bsz, seq, GDN_W)


def diff_attention(q, k, v, lam_q1, lam_k1, lam_q2, lam_k2, norm_g, lambda_init, cos, sin):
    bsz, seq, _ = q.shape
    q = apply_rope(q.reshape(bsz, seq, 2 * DIFF_HEADS, DIFF_DH), cos, sin)
    k = apply_rope(k.reshape(bsz, seq, 2 * DIFF_HEADS, DIFF_DH), cos, sin)
    v = v.reshape(bsz, seq, DIFF_HEADS, DIFF_DV)
    lam = (jnp.exp(jnp.sum(lam_q1 * lam_k1)) - jnp.exp(jnp.sum(lam_q2 * lam_k2))
           + lambda_init).astype(jnp.float32)
    nb = seq // Q_BLOCK
    qb = q.reshape(bsz, nb, Q_BLOCK, 2 * DIFF_HEADS, DIFF_DH).transpose(1, 0, 3, 2, 4) * (DIFF_DH ** -0.5)
    kt = k.transpose(0, 2, 1, 3)
    vt = v.transpose(0, 2, 1, 3)
    key_chunk = jnp.arange(seq) // CHUNK

    def block(args):
        q_blk, blk = args
        scores = jnp.einsum('bhqd,bhkd->bhqk', q_blk, kt)
        q_chunk = (blk * Q_BLOCK + jnp.arange(Q_BLOCK)) // CHUNK
        allowed = key_chunk[None, :] <= q_chunk[:, None]
        p = jax.nn.softmax(jnp.where(allowed, scores, -jnp.inf), axis=-1)
        p = p.reshape(bsz, DIFF_HEADS, 2, Q_BLOCK, seq)
        return jnp.einsum('bhqk,bhkd->bhqd', p[:, :, 0] - lam * p[:, :, 1], vt)

    o = lax.map(block, (qb, jnp.arange(nb)))
    o = o.transpose(1, 0, 3, 2, 4).reshape(bsz, seq, DIFF_HEADS, DIFF_DV)
    o = rmsnorm(o, norm_g) * (1.0 - lambda_init)
    return o.reshape(bsz, seq, DIFF_W)


def ssd_mixer(z, xbc, dt_raw, conv_w, conv_b, a_log, dt_bias, d_skip, norm_g):
    bsz, seq, _ = z.shape
    xbc = jax.nn.silu(causal_dwconv(xbc, conv_w) + conv_b)
    xs, bm, cm = jnp.split(xbc, [SSD_W, SSD_W + SSD_G * SSD_N], axis=-1)
    x = xs.reshape(bsz, seq, SSD_HEADS, SSD_P)
    rep = SSD_HEADS // SSD_G
    bm = jnp.repeat(bm.reshape(bsz, seq, SSD_G, SSD_N), rep, axis=2)
    cm = jnp.repeat(cm.reshape(bsz, seq, SSD_G, SSD_N), rep, axis=2)
    dt = jax.nn.softplus(dt_raw + dt_bias)
    da = dt * (-jnp.exp(a_log))
    nch = seq // CHUNK
    xc = (x * dt[..., None]).reshape(bsz, nch, CHUNK, SSD_HEADS, SSD_P)
    bc = bm.reshape(bsz, nch, CHUNK, SSD_HEADS, SSD_N)
    cc = cm.reshape(bsz, nch, CHUNK, SSD_HEADS, SSD_N)
    acs = jnp.cumsum(da.reshape(bsz, nch, CHUNK, SSD_HEADS).transpose(0, 3, 1, 2), axis=-1)
    l_mat = masked_decay(acs)
    y_diag = jnp.einsum('bclhn,bcmhn,bhclm,bcmhp->bclhp', cc, bc, l_mat, xc)
    decay_states = jnp.exp(acs[..., -1:] - acs)
    states = jnp.einsum('bclhn,bhcl,bclhp->bchpn', bc, decay_states, xc)

    def step(st, inp):
        s_c, d_c = inp
        return st * d_c[..., None, None] + s_c, st

    s0 = jnp.zeros((bsz, SSD_HEADS, SSD_P, SSD_N), jnp.float32)
    _, prev = lax.scan(step, s0, (states.transpose(1, 0, 2, 3, 4),
                                  jnp.exp(acs[..., -1]).transpose(2, 0, 1)))
    prev = prev.transpose(1, 0, 2, 3, 4)
    y_off = jnp.einsum('bclhn,bchpn,bhcl->bclhp', cc, prev, jnp.exp(acs))
    y = (y_diag + y_off).reshape(bsz, seq, SSD_HEADS, SSD_P) + x * d_skip[:, None]
    y = y.reshape(bsz, seq, SSD_W) * jax.nn.silu(z)
    gw = SSD_W // SSD_G
    y = rmsnorm(y.reshape(bsz, seq, SSD_G, gw), norm_g.reshape(SSD_G, gw))
    return y.reshape(bsz, seq, SSD_W)


def mlstm(q, k, v, o_raw, i_raw, f_raw, i_bias, f_bias, norm_g):
    bsz, seq, _ = q.shape
    shp = (bsz, seq, MLSTM_HEADS, MLSTM_DH)
    qc = heads_to_chunks(q.reshape(shp))
    kc = heads_to_chunks(k.reshape(shp) * (MLSTM_DH ** -0.5))
    vc = heads_to_chunks(v.reshape(shp))
    li = heads_to_chunks(i_raw + i_bias)
    lf = heads_to_chunks(jax.nn.log_sigmoid(f_raw + f_bias))
    b = jnp.cumsum(lf, axis=-1)
    d_log = jnp.where(causal_tril(), b[..., :, None] - b[..., None, :] + li[..., None, :], -jnp.inf)
    d_max = jnp.max(d_log, axis=-1)
    qk = jnp.einsum('nbhcd,nbhmd->nbhcm', qc, kc)
    g_end = b[..., -1:] - b + li

    def step(carry, inp):
        c_st, n_st, m_st = carry
        q_i, k_i, v_i, b_i, dl_i, dm_i, qk_i, ge_i = inp
        m_t = jnp.maximum(b_i + m_st[..., None], dm_i)
        w_st = jnp.exp(b_i + m_st[..., None] - m_t)
        s_i = qk_i * jnp.exp(dl_i - m_t[..., None])
        num = w_st[..., None] * (q_i @ c_st) + s_i @ v_i
        den = w_st * jnp.einsum('bhcd,bhd->bhc', q_i, n_st) + jnp.sum(s_i, axis=-1)
        h_i = num / jnp.maximum(jnp.abs(den), jnp.exp(-m_t))[..., None]
        m_new = jnp.maximum(b_i[..., -1] + m_st, jnp.max(ge_i, axis=-1))
        w_old = jnp.exp(b_i[..., -1] + m_st - m_new)
        k_w = k_i * jnp.exp(ge_i - m_new[..., None])[..., None]
        c_st = c_st * w_old[..., None, None] + jnp.einsum('bhcd,bhce->bhde', k_w, v_i)
        n_st = n_st * w_old[..., None] + jnp.sum(k_w, axis=-2)
        return (c_st, n_st, m_new), h_i

    init = (jnp.zeros((bsz, MLSTM_HEADS, MLSTM_DH, MLSTM_DH), jnp.float32),
            jnp.zeros((bsz, MLSTM_HEADS, MLSTM_DH), jnp.float32),
            jnp.zeros((bsz, MLSTM_HEADS), jnp.float32))
    _, h = lax.scan(step, init, (qc, kc, vc, b, d_log, d_max, qk, g_end))
    h = chunks_to_heads(h) * jax.nn.sigmoid(o_raw.reshape(shp))
    h = rmsnorm(h, norm_g.reshape(MLSTM_HEADS, MLSTM_DH))
    return h.reshape(bsz, seq, MLSTM_W)


def hybrid_mixer(h, w_in, w_out, gdn_conv_w, gdn_a_log, gdn_dt_bias, gdn_norm_g,
                 diff_lam_q1, diff_lam_k1, diff_lam_q2, diff_lam_k2, diff_norm_g,
                 ssd_conv_w, ssd_conv_b, ssd_a_log, ssd_dt_bias, ssd_d, ssd_norm_g,
                 mlstm_i_bias, mlstm_f_bias, mlstm_norm_g, lambda_init, cos, sin):
    proj = (h @ w_in).astype(jnp.float32)
    points = np.cumsum(np.array(IN_SPLITS))[:-1].tolist()
    (a_q, a_k, a_v, a_g, a_beta, a_a,
     b_q, b_k, b_v,
     c_z, c_xbc, c_dt,
     d_q, d_k, d_v, d_o, d_i, d_f) = jnp.split(proj, points, axis=-1)
    out_a = gated_deltanet(a_q, a_k, a_v, a_g, a_beta, a_a, gdn_conv_w, gdn_a_log, gdn_dt_bias, gdn_norm_g)
    out_b = diff_attention(b_q, b_k, b_v, diff_lam_q1, diff_lam_k1, diff_lam_q2, diff_lam_k2,
                           diff_norm_g, lambda_init, cos, sin)
    out_c = ssd_mixer(c_z, c_xbc, c_dt, ssd_conv_w, ssd_conv_b, ssd_a_log, ssd_dt_bias, ssd_d, ssd_norm_g)
    out_d = mlstm(d_q, d_k, d_v, d_o, d_i, d_f, mlstm_i_bias, mlstm_f_bias, mlstm_norm_g)
    mixed = jnp.concatenate([out_a, out_b, out_c, out_d], axis=-1)
    return mixed.astype(h.dtype) @ w_out


def setup_inputs(seed: int = 0) -> dict:
    key = jax.random.key(seed)
    keys = list(jax.random.split(key, 48))
    f32 = jnp.float32
    L = DEPTH

    def nxt():
        return keys.pop()

    def normal(shape, scale):
        return jax.random.normal(nxt(), shape, f32) * scale

    def gain(n):
        return 1.0 + normal((L, n), 0.02)

    def a_log_init(n):
        return jnp.log(jax.random.uniform(nxt(), (L, n), f32, 1.0, 16.0))

    def dt_bias_init(n):
        dt = jnp.exp(jax.random.uniform(nxt(), (L, n), f32, math.log(1e-3), math.log(1e-1)))
        return dt + jnp.log(-jnp.expm1(-dt))

    xbc_w = SSD_W + 2 * SSD_G * SSD_N
    return {
        'x': normal((BATCH, SEQ, D_MODEL), 1.0),
        'ffn1_pre_g': gain(D_MODEL),
        'ffn1_w_gate': normal((L, D_MODEL, D_FF), D_MODEL ** -0.5),
        'ffn1_w_up': normal((L, D_MODEL, D_FF), D_MODEL ** -0.5),
        'ffn1_w_down': normal((L, D_FF, D_MODEL), D_FF ** -0.5),
        'ffn1_post_g': gain(D_MODEL),
        'mix_pre_g': gain(D_MODEL),
        'w_in': normal((L, D_MODEL, IN_COLS), D_MODEL ** -0.5),
        'gdn_conv_w': normal((L, CONV_K, 2 * GDN_QK_W + GDN_W), CONV_K ** -0.5),
        'gdn_a_log': a_log_init(GDN_HEADS),
        'gdn_dt_bias': dt_bias_init(GDN_HEADS),
        'gdn_norm_g': gain(GDN_DV),
        'diff_lam_q1': normal((L, DIFF_DH), 0.1),
        'diff_lam_k1': normal((L, DIFF_DH), 0.1),
        'diff_lam_q2': normal((L, DIFF_DH), 0.1),
        'diff_lam_k2': normal((L, DIFF_DH), 0.1),
        'diff_norm_g': gain(DIFF_DV),
        'ssd_conv_w': normal((L, CONV_K, xbc_w), CONV_K ** -0.5),
        'ssd_conv_b': normal((L, xbc_w), 0.02),
        'ssd_a_log': a_log_init(SSD_HEADS),
        'ssd_dt_bias': dt_bias_init(SSD_HEADS),
        'ssd_d': 1.0 + normal((L, SSD_HEADS), 0.1),
        'ssd_norm_g': gain(SSD_W),
        'mlstm_i_bias': normal((L, MLSTM_HEADS), 0.1),
        'mlstm_f_bias': jnp.linspace(3.0, 6.0, MLSTM_HEADS, dtype=f32)[None, :] + normal((L, MLSTM_HEADS), 0.1),
        'mlstm_norm_g': gain(MLSTM_W),
        'w_out': normal((L, D_MIX, D_MODEL), D_MIX ** -0.5),
        'mix_post_g': gain(D_MODEL),
        'ffn2_pre_g': gain(D_MODEL),
        'ffn2_w_gate': normal((L, D_MODEL, D_FF), D_MODEL ** -0.5),
        'ffn2_w_up': normal((L, D_MODEL, D_FF), D_MODEL ** -0.5),
        'ffn2_w_down': normal((L, D_FF, D_MODEL), D_FF ** -0.5),
        'ffn2_post_g': gain(D_MODEL),
    }


def reference(x, ffn1_pre_g, ffn1_w_gate, ffn1_w_up, ffn1_w_down, ffn1_post_g,
              mix_pre_g, w_in,
              gdn_conv_w, gdn_a_log, gdn_dt_bias, gdn_norm_g,
              diff_lam_q1, diff_lam_k1, diff_lam_q2, diff_lam_k2, diff_norm_g,
              ssd_conv_w, ssd_conv_b, ssd_a_log, ssd_dt_bias, ssd_d, ssd_norm_g,
              mlstm_i_bias, mlstm_f_bias, mlstm_norm_g,
              w_out, mix_post_g,
              ffn2_pre_g, ffn2_w_gate, ffn2_w_up, ffn2_w_down, ffn2_post_g):
    cos, sin = rope_tables(x.shape[1], DIFF_DH)
    for l in range(DEPTH):
        lambda_init = 0.8 - 0.6 * math.exp(-0.3 * l)
        h = rmsnorm(x, ffn1_pre_g[l])
        x = x + 0.5 * rmsnorm(swiglu(h, ffn1_w_gate[l], ffn1_w_up[l], ffn1_w_down[l]), ffn1_post_g[l])
        h = rmsnorm(x, mix_pre_g[l])
        m = hybrid_mixer(h, w_in[l], w_out[l],
                         gdn_conv_w[l], gdn_a_log[l], gdn_dt_bias[l], gdn_norm_g[l],
                         diff_lam_q1[l], diff_lam_k1[l], diff_lam_q2[l], diff_lam_k2[l], diff_norm_g[l],
                         ssd_conv_w[l], ssd_conv_b[l], ssd_a_log[l], ssd_dt_bias[l], ssd_d[l], ssd_norm_g[l],
                         mlstm_i_bias[l], mlstm_f_bias[l], mlstm_norm_g[l],
                         lambda_init, cos, sin)
        x = x + rmsnorm(m, mix_post_g[l])
        h = rmsnorm(x, ffn2_pre_g[l])
        x = x + 0.5 * rmsnorm(swiglu(h, ffn2_w_gate[l], ffn2_w_up[l], ffn2_w_down[l]), ffn2_post_g[l])
    return x
```

```python
import functools
import math

import jax
import jax.numpy as jnp
from jax import lax
from jax.experimental import pallas as pl
from jax.experimental.pallas import tpu as pltpu

F32 = jnp.float32
MXU_DTYPE = jnp.bfloat16
HI = lax.Precision.HIGHEST

D_MODEL = 1024
D_FF = 2816
DEPTH = 2
CHUNK = 64
CONV_K = 4
EPS = 1e-6
ROPE_THETA = 10000.0
HEADS = 4
HEAD_W = 64
MIX_W = HEADS * HEAD_W
DIFF_DH = 32
SSD_N = 128
LANES = 128
NEG = -1e30

_GDN_COLS = (0, 1032)
_DIFF_COLS = (1032, 1800)
_SSD_COLS = (1800, 2828)
_MLSTM_COLS = (2828, 3860)
SLAB_W = 1152

FFN_TM = 512
FFN_FC = 256
MIX_T = 256
ATT_TQ = 128
ATT_TK = 512
VMEM_LIMIT = 56 * 1024 * 1024


def _dot(a, b):
    return jnp.dot(a.astype(MXU_DTYPE), b.astype(MXU_DTYPE), preferred_element_type=F32)


def _dot_nt(a, b):
    return lax.dot_general(a.astype(MXU_DTYPE), b.astype(MXU_DTYPE), (((1,), (1,)), ((), ())),
                           preferred_element_type=F32)


def _dot_tn(a, b):
    return lax.dot_general(a.astype(MXU_DTYPE), b.astype(MXU_DTYPE), (((0,), (0,)), ((), ())),
                           preferred_element_type=F32)


def _dot_hi(a, b):
    return jnp.dot(a, b, preferred_element_type=F32, precision=HI)


def _rms(x, g):
    return x * lax.rsqrt(jnp.mean(x * x, axis=-1, keepdims=True) + EPS) * g


def _silu(x):
    return x / (1.0 + jnp.exp(-x))


def _sigmoid(x):
    return 1.0 / (1.0 + jnp.exp(-x))


def _softplus(x):
    return jnp.maximum(x, 0.0) + jnp.log(1.0 + jnp.exp(-jnp.abs(x)))


def _iota(shape, dim):
    return lax.broadcasted_iota(jnp.int32, shape, dim)


def _chunk_consts(t):
    r = _iota((t, t), 0)
    c = _iota((t, t), 1)
    same = (r >> 6) == (c >> 6)
    ltile = jnp.where(same & (c <= r), 1.0, 0.0).astype(F32)
    lastsel = jnp.where(c == ((r >> 6) << 6) + (CHUNK - 1), 1.0, 0.0).astype(F32)
    csame = jnp.where(same, 1.0, 0.0).astype(F32)
    pos = _iota((t, MIX_W), 0) & (CHUNK - 1)
    m = _iota((t, MIX_W), 1) & (CHUNK - 1)
    return ltile, lastsel, csame, pos, m


def _head_expand(first_lane):
    r = _iota((LANES, MIX_W), 0)
    c = _iota((LANES, MIX_W), 1)
    return jnp.where(r == (c >> 6) + first_lane, 1.0, 0.0).astype(F32)


def _head_ones():
    r = _iota((MIX_W, MIX_W), 0)
    c = _iota((MIX_W, MIX_W), 1)
    return jnp.where((r >> 6) == (c >> 6), 1.0, 0.0).astype(F32)


def _causal_conv(u, w_ref, carry_ref, buf_ref, t):
    buf_ref[0:8, :] = carry_ref[...]
    buf_ref[8:8 + t, :] = u
    carry_ref[...] = u[t - 8:t, :]
    acc = w_ref[CONV_K - 1:CONV_K, :] * u
    for j in range(CONV_K - 1):
        off = 8 - (CONV_K - 1) + j
        acc = acc + w_ref[j:j + 1, :] * buf_ref[off:off + t, :]
    return acc


def _ffn_kernel(x_ref, pre_ref, wgu_ref, wd_ref, post_ref, o_ref, h_ref, acc_ref, *, n_chunks, fc):
    x = x_ref[...]
    h_ref[...] = _rms(x, pre_ref[...]).astype(h_ref.dtype)
    acc_ref[...] = jnp.zeros_like(acc_ref)

    def body(c, carry):
        r = jnp.dot(h_ref[...], wgu_ref[c], preferred_element_type=F32)
        a = _silu(r[:, :fc]) * r[:, fc:]
        acc_ref[...] += jnp.dot(a.astype(wd_ref.dtype), wd_ref[c], preferred_element_type=F32)
        return carry

    lax.fori_loop(0, n_chunks, body, 0)
    o_ref[...] = x + 0.5 * _rms(acc_ref[...], post_ref[...])


def _ffn_call(x2, pre_g, w_gate, w_up, w_down, post_g, tm=FFN_TM, fc=FFN_FC):
    m, d = x2.shape
    ff = w_gate.shape[1]
    n_chunks = ff // fc
    wg = w_gate.reshape(d, n_chunks, fc).transpose(1, 0, 2)
    wu = w_up.reshape(d, n_chunks, fc).transpose(1, 0, 2)
    wgu = jnp.concatenate([wg, wu], axis=-1).astype(MXU_DTYPE)
    wd = w_down.reshape(n_chunks, fc, d).astype(MXU_DTYPE)
    const3 = lambda i: (0, 0, 0)
    return pl.pallas_call(
        functools.partial(_ffn_kernel, n_chunks=n_chunks, fc=fc),
        out_shape=jax.ShapeDtypeStruct((m, d), F32),
        grid=(m // tm,),
        in_specs=[
            pl.BlockSpec((tm, d), lambda i: (i, 0)),
            pl.BlockSpec((1, d), lambda i: (0, 0)),
            pl.BlockSpec((n_chunks, d, 2 * fc), const3),
            pl.BlockSpec((n_chunks, fc, d), const3),
            pl.BlockSpec((1, d), lambda i: (0, 0)),
        ],
        out_specs=pl.BlockSpec((tm, d), lambda i: (i, 0)),
        scratch_shapes=[pltpu.VMEM((tm, d), MXU_DTYPE), pltpu.VMEM((tm, d), F32)],
        compiler_params=pltpu.CompilerParams(dimension_semantics=("parallel",), vmem_limit_bytes=VMEM_LIMIT),
        name="ffn",
    )(x2, pre_g.reshape(1, d), wgu, wd, post_g.reshape(1, d))


def _rope_half(x, cos, sin):
    lane = _iota(x.shape, 1)
    partner = jnp.where((lane & 31) < 16, pltpu.roll(x, LANES - 16, 1), pltpu.roll(x, 16, 1))
    return x * cos + partner * sin


def _proj_kernel(x_ref, g_ref, wa_ref, wb_ref, wc_ref, wd_ref, cos_ref, sin_ref,
                 gdn_ref, ssd_ref, ml_ref, dq_ref, dk_ref, dv_ref):
    h = _rms(x_ref[...], g_ref[...]).astype(wa_ref.dtype)
    gdn_ref[...] = jnp.dot(h, wa_ref[...], preferred_element_type=F32)
    ssd_ref[...] = jnp.dot(h, wc_ref[...], preferred_element_type=F32)
    ml_ref[...] = jnp.dot(h, wd_ref[...], preferred_element_type=F32)
    d = jnp.dot(h, wb_ref[...], preferred_element_type=F32)
    cos = cos_ref[...]
    sin = sin_ref[...]
    scale = DIFF_DH ** -0.5
    for half in range(2):
        lo = half * LANES
        dq_ref[:, lo:lo + LANES] = (_rope_half(d[:, lo:lo + LANES], cos, sin) * scale).astype(dq_ref.dtype)
        dk_ref[:, lo:lo + LANES] = _rope_half(d[:, MIX_W + lo:MIX_W + lo + LANES], cos, sin).astype(dk_ref.dtype)
    dv_ref[...] = d[:, 2 * MIX_W:].astype(dv_ref.dtype)


def _pad_cols(w, width):
    return jnp.pad(w, ((0, 0), (0, width - w.shape[1])))


def _proj_call(x2, pre_g, w_in, cos_t, sin_t, seq, tm=FFN_TM):
    m, d = x2.shape
    wa = _pad_cols(w_in[:, _GDN_COLS[0]:_GDN_COLS[1]], SLAB_W).astype(MXU_DTYPE)
    wb = w_in[:, _DIFF_COLS[0]:_DIFF_COLS[1]].astype(MXU_DTYPE)
    wc = _pad_cols(w_in[:, _SSD_COLS[0]:_SSD_COLS[1]], SLAB_W).astype(MXU_DTYPE)
    wd = _pad_cols(w_in[:, _MLSTM_COLS[0]:_MLSTM_COLS[1]], SLAB_W).astype(MXU_DTYPE)
    tiles_per_seq = seq // tm
    full = lambda i: (0, 0)
    row = lambda i: (i, 0)
    return pl.pallas_call(
        _proj_kernel,
        out_shape=(
            jax.ShapeDtypeStruct((m, SLAB_W), F32),
            jax.ShapeDtypeStruct((m, SLAB_W), F32),
            jax.ShapeDtypeStruct((m, SLAB_W), F32),
            jax.ShapeDtypeStruct((m, MIX_W), MXU_DTYPE),
            jax.ShapeDtypeStruct((m, MIX_W), MXU_DTYPE),
            jax.ShapeDtypeStruct((m, MIX_W), MXU_DTYPE),
        ),
        grid=(m // tm,),
        in_specs=[
            pl.BlockSpec((tm, d), row),
            pl.BlockSpec((1, d), full),
            pl.BlockSpec((d, SLAB_W), full),
            pl.BlockSpec((d, 3 * MIX_W), full),
            pl.BlockSpec((d, SLAB_W), full),
            pl.BlockSpec((d, SLAB_W), full),
            pl.BlockSpec((tm, LANES), lambda i: (i % tiles_per_seq, 0)),
            pl.BlockSpec((tm, LANES), lambda i: (i % tiles_per_seq, 0)),
        ],
        out_specs=(
            pl.BlockSpec((tm, SLAB_W), row),
            pl.BlockSpec((tm, SLAB_W), row),
            pl.BlockSpec((tm, SLAB_W), row),
            pl.BlockSpec((tm, MIX_W), row),
            pl.BlockSpec((tm, MIX_W), row),
            pl.BlockSpec((tm, MIX_W), row),
        ),
        compiler_params=pltpu.CompilerParams(dimension_semantics=("parallel",), vmem_limit_bytes=VMEM_LIMIT),
        name="mixer_in_proj",
    )(x2, pre_g.reshape(1, d), wa, wb, wc, wd, cos_t, sin_t)


def _outproj_kernel(x_ref, a_ref, b_ref, c_ref, d_ref, w_ref, g_ref, o_ref):
    mixed = jnp.concatenate([a_ref[...], b_ref[...], c_ref[...], d_ref[...]], axis=-1).astype(w_ref.dtype)
    y = jnp.dot(mixed, w_ref[...], preferred_element_type=F32)
    o_ref[...] = x_ref[...] + _rms(y, g_ref[...])


def _outproj_call(x2, outs, w_out, post_g, tm=FFN_TM):
    m, d = x2.shape
    row = lambda i: (i, 0)
    full = lambda i: (0, 0)
    return pl.pallas_call(
        _outproj_kernel,
        out_shape=jax.ShapeDtypeStruct((m, d), F32),
        grid=(m // tm,),
        in_specs=[pl.BlockSpec((tm, d), row)] + [pl.BlockSpec((tm, MIX_W), row)] * 4
        + [pl.BlockSpec((d, d), full), pl.BlockSpec((1, d), full)],
        out_specs=pl.BlockSpec((tm, d), row),
        compiler_params=pltpu.CompilerParams(dimension_semantics=("parallel",), vmem_limit_bytes=VMEM_LIMIT),
        name="mixer_out_proj",
    )(x2, *outs, w_out.astype(MXU_DTYPE), post_g.reshape(1, d))


def _attn_kernel(q_ref, k_ref, v_ref, lam_ref, g_ref, o_ref, qs_ref, m_ref, l_ref, acc_ref,
                 *, tq, tk, lambda_init):
    i = pl.program_id(1)
    n_maps = 2 * HEADS
    rows = n_maps * tq
    q = q_ref[0]
    lane = _iota((tq, MIX_W), 1)
    for mp in range(n_maps):
        qs_ref[mp * tq:(mp + 1) * tq, :] = jnp.where((lane >> 5) == mp, q, jnp.zeros_like(q))
    m_ref[...] = jnp.full_like(m_ref, NEG)
    l_ref[...] = jnp.zeros_like(l_ref)
    acc_ref[...] = jnp.zeros_like(acc_ref)
    qpos = i * tq + (_iota((rows, 1), 0) & (tq - 1))
    limit = ((qpos >> 6) + 1) << 6
    n_kt = (i * tq + tq + tk - 1) // tk

    def body(j, carry):
        start = pl.multiple_of(j * tk, tk)
        kt = k_ref[0, pl.ds(start, tk), :]
        vt = v_ref[0, pl.ds(start, tk), :]
        s = lax.dot_general(qs_ref[...], kt, (((1,), (1,)), ((), ())), preferred_element_type=F32)
        key = start + _iota((1, tk), 1)
        s = jnp.where(key < limit, s, NEG)
        m_old = m_ref[...]
        m_new = jnp.maximum(m_old, jnp.max(s, axis=-1, keepdims=True))
        alpha = jnp.exp(m_old - m_new)
        p = jnp.exp(s - m_new)
        l_ref[...] = alpha * l_ref[...] + jnp.sum(p, axis=-1, keepdims=True)
        acc_ref[...] = alpha * acc_ref[...] + jnp.dot(p.astype(vt.dtype), vt, preferred_element_type=F32)
        m_ref[...] = m_new
        return carry

    lax.fori_loop(0, n_kt, body, 0)

    lam_p = lam_ref[...]
    lam = (jnp.exp(jnp.sum(lam_p[0:1] * lam_p[1:2], axis=-1, keepdims=True))
           - jnp.exp(jnp.sum(lam_p[2:3] * lam_p[3:4], axis=-1, keepdims=True)) + lambda_init)
    inv_l = 1.0 / l_ref[...]
    outs = []
    for h in range(HEADS):
        r1 = (2 * h) * tq
        r2 = (2 * h + 1) * tq
        a1 = acc_ref[r1:r1 + tq, h * HEAD_W:(h + 1) * HEAD_W] * inv_l[r1:r1 + tq]
        a2 = acc_ref[r2:r2 + tq, h * HEAD_W:(h + 1) * HEAD_W] * inv_l[r2:r2 + tq]
        o = a1 - lam * a2
        o = o * lax.rsqrt(jnp.mean(o * o, axis=-1, keepdims=True) + EPS)
        outs.append(o)
    o_ref[0] = jnp.concatenate(outs, axis=-1) * g_ref[...] * (1.0 - lambda_init)


def _attn_call(dq, dk, dv, lam_p, norm_g, lambda_init, bsz, seq, tq=ATT_TQ, tk=ATT_TK):
    tk = min(tk, seq)
    q3 = dq.reshape(bsz, seq, MIX_W)
    k3 = dk.reshape(bsz, seq, MIX_W)
    v3 = dv.reshape(bsz, seq, MIX_W)
    rows = 2 * HEADS * tq
    out = pl.pallas_call(
        functools.partial(_attn_kernel, tq=tq, tk=tk, lambda_init=lambda_init),
        out_shape=jax.ShapeDtypeStruct((bsz, seq, MIX_W), F32),
        grid=(bsz, seq // tq),
        in_specs=[
            pl.BlockSpec((1, tq, MIX_W), lambda b, i: (b, i, 0)),
            pl.BlockSpec((1, seq, MIX_W), lambda b, i: (b, 0, 0)),
            pl.BlockSpec((1, seq, MIX_W), lambda b, i: (b, 0, 0)),
            pl.BlockSpec((4, DIFF_DH), lambda b, i: (0, 0)),
            pl.BlockSpec((1, MIX_W), lambda b, i: (0, 0)),
        ],
        out_specs=pl.BlockSpec((1, tq, MIX_W), lambda b, i: (b, i, 0)),
        scratch_shapes=[
            pltpu.VMEM((rows, MIX_W), MXU_DTYPE),
            pltpu.VMEM((rows, 1), F32),
            pltpu.VMEM((rows, 1), F32),
            pltpu.VMEM((rows, MIX_W), F32),
        ],
        compiler_params=pltpu.CompilerParams(dimension_semantics=("parallel", "arbitrary"),
                                             vmem_limit_bytes=VMEM_LIMIT),
        name="diff_attention",
    )(q3, k3, v3, lam_p, jnp.tile(norm_g, HEADS).reshape(1, MIX_W))
    return out.reshape(bsz * seq, MIX_W)


def _ssd_kernel(slab_ref, cw_ref, cb_ref, alog_ref, dtb_ref, dskip_ref, g_ref, y_ref,
                carry_ref, buf_ref, st_ref, xc_ref, xcd_ref, bm_ref, cm_ref, lm_ref, ea_ref, el_ref, yo_ref, *, t):
    @pl.when(pl.program_id(1) == 0)
    def _():
        carry_ref[...] = jnp.zeros_like(carry_ref)
        st_ref[...] = jnp.zeros_like(st_ref)

    ltile, lastsel, _, pos, mm = _chunk_consts(t)
    expand = _head_expand(0)
    z = slab_ref[0, :, 0:MIX_W]
    xbc = _silu(_causal_conv(slab_ref[0, :, MIX_W:4 * MIX_W], cw_ref, carry_ref, buf_ref, t) + cb_ref[...])
    x = xbc[:, 0:MIX_W]
    bm_ref[...] = xbc[:, MIX_W:2 * MIX_W].astype(bm_ref.dtype)
    cm_ref[...] = xbc[:, 2 * MIX_W:3 * MIX_W].astype(cm_ref.dtype)
    dt = _softplus(slab_ref[0, :, 4 * MIX_W:] + dtb_ref[...])
    da = dt * (-jnp.exp(alog_ref[...]))
    dt_e = _dot_hi(dt, expand)
    da_e = _dot_hi(da, expand)
    acs_e = _dot_hi(ltile, da_e)
    acs_last_e = _dot_hi(lastsel, acs_e)
    ldiff = _dot_hi(ltile, jnp.where(pos > mm, da_e, 0.0))
    lm_ref[...] = jnp.where(mm <= pos, jnp.exp(ldiff), 0.0)
    xc = x * dt_e
    xc_ref[...] = xc.astype(xc_ref.dtype)
    xcd_ref[...] = (xc * jnp.exp(acs_last_e - acs_e)).astype(xcd_ref.dtype)
    ea_ref[...] = jnp.exp(acs_e)
    el_ref[...] = jnp.exp(acs_last_e)

    for c in range(t // CHUNK):
        r0 = c * CHUNK
        rows = slice(r0, r0 + CHUNK)
        for grp in range(2):
            gl = slice(grp * SSD_N, (grp + 1) * SSD_N)
            bg = bm_ref[rows, gl]
            cg = cm_ref[rows, gl]
            cb = _dot_nt(cg, bg)
            ys = []
            for hh in range(2):
                hl = slice(grp * SSD_N + hh * HEAD_W, grp * SSD_N + (hh + 1) * HEAD_W)
                ys.append(_dot(cb * lm_ref[rows, hl], xc_ref[rows, hl]))
            y_diag = jnp.concatenate(ys, axis=-1)
            st = st_ref[grp]
            y_off = _dot(cg, st) * ea_ref[rows, gl]
            st_ref[grp] = st * el_ref[r0:r0 + 1, gl] + _dot_tn(bg, xcd_ref[rows, gl])
            yo_ref[rows, gl] = y_diag + y_off

    y = (yo_ref[...] + x * dskip_ref[...]) * _silu(z)
    gs = g_ref[...]
    for grp in range(2):
        gl = slice(grp * SSD_N, (grp + 1) * SSD_N)
        y_ref[0, :, gl] = _rms(y[:, gl], gs[:, gl])


def _ssd_call(slab, conv_w, conv_b, a_log, dt_bias, d_skip, norm_g, bsz, seq, t=MIX_T):
    ch = 3 * MIX_W
    lane_pad = lambda v: jnp.pad(v, (0, LANES - v.shape[0])).reshape(1, LANES)
    full = lambda b, i: (0, 0)
    out = pl.pallas_call(
        functools.partial(_ssd_kernel, t=t),
        out_shape=jax.ShapeDtypeStruct((bsz, seq, MIX_W), F32),
        grid=(bsz, seq // t),
        in_specs=[
            pl.BlockSpec((1, t, SLAB_W), lambda b, i: (b, i, 0)),
            pl.BlockSpec((CONV_K, ch), full),
            pl.BlockSpec((1, ch), full),
            pl.BlockSpec((1, LANES), full),
            pl.BlockSpec((1, LANES), full),
            pl.BlockSpec((1, MIX_W), full),
            pl.BlockSpec((1, MIX_W), full),
        ],
        out_specs=pl.BlockSpec((1, t, MIX_W), lambda b, i: (b, i, 0)),
        scratch_shapes=[
            pltpu.VMEM((8, ch), F32),
            pltpu.VMEM((t + 8, ch), F32),
            pltpu.VMEM((2, SSD_N, SSD_N), F32),
            pltpu.VMEM((t, MIX_W), MXU_DTYPE),
            pltpu.VMEM((t, MIX_W), MXU_DTYPE),
            pltpu.VMEM((t, MIX_W), MXU_DTYPE),
            pltpu.VMEM((t, MIX_W), MXU_DTYPE),
            pltpu.VMEM((t, MIX_W), F32),
            pltpu.VMEM((t, MIX_W), F32),
            pltpu.VMEM((t, MIX_W), F32),
            pltpu.VMEM((t, MIX_W), F32),
        ],
        compiler_params=pltpu.CompilerParams(dimension_semantics=("parallel", "arbitrary"),
                                             vmem_limit_bytes=VMEM_LIMIT),
        name="ssd_mixer",
    )(slab.reshape(bsz, seq, SLAB_W), conv_w, conv_b.reshape(1, ch), lane_pad(a_log), lane_pad(dt_bias),
      jnp.repeat(d_skip, HEAD_W).reshape(1, MIX_W), norm_g.reshape(1, MIX_W))
    return out.reshape(bsz * seq, MIX_W)


def _gdn_kernel(slab_ref, cw_ref, alog_ref, dtb_ref, g_ref, y_ref,
                carry_ref, buf_ref, st_ref, qn_ref, kn_ref, qd_ref, kd_ref, kb_ref, vk_ref, dec_ref, egl_ref,
                o_ref, *, t):
    @pl.when(pl.program_id(1) == 0)
    def _():
        carry_ref[...] = jnp.zeros_like(carry_ref)
        st_ref[...] = jnp.zeros_like(st_ref)

    ltile, lastsel, _, pos, mm = _chunk_consts(t)
    ones_h = _head_ones()
    qkv = _silu(_causal_conv(slab_ref[0, :, 0:3 * MIX_W], cw_ref, carry_ref, buf_ref, t))
    q = qkv[:, 0:MIX_W]
    k = qkv[:, MIX_W:2 * MIX_W]
    v = qkv[:, 2 * MIX_W:]
    qn = q * lax.rsqrt(_dot_hi(q * q, ones_h) + EPS) * (HEAD_W ** -0.5)
    kn = k * lax.rsqrt(_dot_hi(k * k, ones_h) + EPS)
    ba = slab_ref[0, :, 4 * MIX_W:]
    beta_e = _dot_hi(_sigmoid(ba), _head_expand(0))
    g4 = -jnp.exp(alog_ref[...]) * _softplus(ba + dtb_ref[...])
    g_e = _dot_hi(g4, _head_expand(HEADS))
    gc_e = _dot_hi(ltile, g_e)
    gl_e = _dot_hi(lastsel, gc_e)
    diff = _dot_hi(ltile, jnp.where(pos > mm, g_e, 0.0))
    dec_ref[...] = jnp.where(mm <= pos, jnp.exp(diff), 0.0)
    egc = jnp.exp(gc_e)
    kb = kn * beta_e
    qn_ref[...] = qn
    kn_ref[...] = kn
    qd_ref[...] = qn * egc
    kd_ref[...] = kn * jnp.exp(gl_e - gc_e)
    kb_ref[...] = kb
    egl_ref[...] = jnp.exp(gl_e)

    r64 = _iota((CHUNK, CHUNK), 0)
    c64 = _iota((CHUNK, CHUNK), 1)
    eye = jnp.where(r64 == c64, 1.0, 0.0).astype(F32)
    strict = c64 < r64
    vb = v * beta_e
    kbg = kb * egc

    for c in range(t // CHUNK):
        r0 = c * CHUNK
        rows = slice(r0, r0 + CHUNK)
        for h in range(HEADS):
            hl = slice(h * HEAD_W, (h + 1) * HEAD_W)
            kn_h = kn_ref[rows, hl]
            dec_h = dec_ref[rows, hl]
            lower = jnp.where(strict, _dot_nt(kb_ref[rows, hl], kn_h) * dec_h, 0.0)
            attn = _dot_nt(qn_ref[rows, hl], kn_h) * dec_h
            y = -lower
            tm = eye + y
            for _ in range(5):
                y = _dot_hi(y, y)
                tm = tm + _dot_hi(tm, y)
            uw = _dot(tm, jnp.concatenate([vb[rows, hl], kbg[rows, hl]], axis=-1))
            st = st_ref[h]
            v_new = uw[:, :HEAD_W] - _dot(uw[:, HEAD_W:], st)
            o_ref[rows, hl] = _dot(qd_ref[rows, hl], st) + _dot(attn, v_new)
            st_ref[h] = st * egl_ref[r0:r0 + 1, hl] + _dot_tn(kd_ref[rows, hl], v_new)

    o = o_ref[...]
    o = o * lax.rsqrt(_dot_hi(o * o, ones_h) * (1.0 / HEAD_W) + EPS) * g_ref[...]
    y_ref[0] = o * _silu(slab_ref[0, :, 3 * MIX_W:4 * MIX_W])


def _gdn_call(slab, conv_w, a_log, dt_bias, norm_g, bsz, seq, t=MIX_T):
    ch = 3 * MIX_W
    gate_pad = lambda v: jnp.pad(v, (HEADS, LANES - 2 * HEADS)).reshape(1, LANES)
    full = lambda b, i: (0, 0)
    tile = lambda: pltpu.VMEM((t, MIX_W), F32)
    out = pl.pallas_call(
        functools.partial(_gdn_kernel, t=t),
        out_shape=jax.ShapeDtypeStruct((bsz, seq, MIX_W), F32),
        grid=(bsz, seq // t),
        in_specs=[
            pl.BlockSpec((1, t, SLAB_W), lambda b, i: (b, i, 0)),
            pl.BlockSpec((CONV_K, ch), full),
            pl.BlockSpec((1, LANES), full),
            pl.BlockSpec((1, LANES), full),
            pl.BlockSpec((1, MIX_W), full),
        ],
        out_specs=pl.BlockSpec((1, t, MIX_W), lambda b, i: (b, i, 0)),
        scratch_shapes=[
            pltpu.VMEM((8, ch), F32),
            pltpu.VMEM((t + 8, ch), F32),
            pltpu.VMEM((HEADS, HEAD_W, HEAD_W), F32),
            tile(), tile(), tile(), tile(), tile(), tile(), tile(), tile(), tile(),
        ],
        compiler_params=pltpu.CompilerParams(dimension_semantics=("parallel", "arbitrary"),
                                             vmem_limit_bytes=VMEM_LIMIT),
        name="gated_deltanet",
    )(slab.reshape(bsz, seq, SLAB_W), conv_w, gate_pad(a_log), gate_pad(dt_bias),
      jnp.tile(norm_g, HEADS).reshape(1, MIX_W))
    return out.reshape(bsz * seq, MIX_W)


def _mlstm_kernel(slab_ref, bias_ref, g_ref, y_ref, c_ref, m_ref, dl_ref, b_ref, bl_ref, ge_ref, o_ref, *, t):
    @pl.when(pl.program_id(1) == 0)
    def _():
        c_ref[...] = jnp.zeros_like(c_ref)
        m_ref[...] = jnp.zeros_like(m_ref)

    ltile, lastsel, csame, pos, mm = _chunk_consts(t)
    gates = slab_ref[0, :, 4 * MIX_W:] + bias_ref[...]
    li_e = _dot_hi(gates, _head_expand(0))
    lf_e = _dot_hi(-_softplus(-gates), _head_expand(HEADS))
    b_e = _dot_hi(ltile, lf_e)
    dlog = _dot_hi(ltile, jnp.where(pos > mm, lf_e, 0.0)) + _dot_hi(csame, jnp.where(pos == mm, li_e, 0.0))
    dl_ref[...] = jnp.where(mm <= pos, dlog, NEG)
    bl_e = _dot_hi(lastsel, b_e)
    b_ref[...] = b_e
    bl_ref[...] = bl_e
    ge_ref[...] = bl_e - b_e + li_e

    ones_col = jnp.where(_iota((CHUNK, HEAD_W), 1) == 0, 1.0, 0.0).astype(F32)
    for c in range(t // CHUNK):
        r0 = c * CHUNK
        rows = slice(r0, r0 + CHUNK)
        for h in range(HEADS):
            hl = slice(h * HEAD_W, (h + 1) * HEAD_W)
            q_h = slab_ref[0, rows, hl]
            k_h = slab_ref[0, rows, MIX_W + h * HEAD_W:MIX_W + (h + 1) * HEAD_W] * (HEAD_W ** -0.5)
            v_aug = jnp.concatenate([slab_ref[0, rows, 2 * MIX_W + h * HEAD_W:2 * MIX_W + (h + 1) * HEAD_W],
                                     ones_col], axis=-1)
            m_st = m_ref[h][0:1, 0:1]
            c_aug = c_ref[h]
            bcol = b_ref[rows, h * HEAD_W:h * HEAD_W + 1]
            dl = dl_ref[rows, hl]
            m_t = jnp.maximum(bcol + m_st, jnp.max(dl, axis=-1, keepdims=True))
            w_st = jnp.exp(bcol + m_st - m_t)
            s = _dot_nt(q_h, k_h) * jnp.exp(dl - m_t)
            nd = w_st * _dot(q_h, c_aug) + _dot(s, v_aug)
            den = nd[:, HEAD_W:HEAD_W + 1]
            o_ref[rows, hl] = nd[:, :HEAD_W] / jnp.maximum(jnp.abs(den), jnp.exp(-m_t))
            ge = ge_ref[rows, h * HEAD_W:h * HEAD_W + 1]
            blast = bl_ref[r0:r0 + 1, h * HEAD_W:h * HEAD_W + 1]
            m_new = jnp.maximum(blast + m_st, jnp.max(ge, axis=0, keepdims=True))
            w_old = jnp.exp(blast + m_st - m_new)
            k_w = k_h * jnp.exp(ge - m_new)
            c_ref[h] = c_aug * w_old + _dot_tn(k_w, v_aug)
            m_ref[h] = jnp.broadcast_to(m_new, (8, LANES))

    hcat = o_ref[...] * _sigmoid(slab_ref[0, :, 3 * MIX_W:4 * MIX_W])
    ms = _dot_hi(hcat * hcat, _head_ones()) * (1.0 / HEAD_W)
    y_ref[0] = hcat * lax.rsqrt(ms + EPS) * g_ref[...]


def _mlstm_call(slab, i_bias, f_bias, norm_g, bsz, seq, t=MIX_T):
    bias = jnp.pad(jnp.concatenate([i_bias, f_bias]), (0, LANES - 2 * HEADS)).reshape(1, LANES)
    full = lambda b, i: (0, 0)
    tile = lambda: pltpu.VMEM((t, MIX_W), F32)
    out = pl.pallas_call(
        functools.partial(_mlstm_kernel, t=t),
        out_shape=jax.ShapeDtypeStruct((bsz, seq, MIX_W), F32),
        grid=(bsz, seq // t),
        in_specs=[
            pl.BlockSpec((1, t, SLAB_W), lambda b, i: (b, i, 0)),
            pl.BlockSpec((1, LANES), full),
            pl.BlockSpec((1, MIX_W), full),
        ],
        out_specs=pl.BlockSpec((1, t, MIX_W), lambda b, i: (b, i, 0)),
        scratch_shapes=[
            pltpu.VMEM((HEADS, HEAD_W, 2 * HEAD_W), F32),
            pltpu.VMEM((HEADS, 8, LANES), F32),
            tile(), tile(), tile(), tile(), tile(),
        ],
        compiler_params=pltpu.CompilerParams(dimension_semantics=("parallel", "arbitrary"),
                                             vmem_limit_bytes=VMEM_LIMIT),
        name="mlstm",
    )(slab.reshape(bsz, seq, SLAB_W), bias, norm_g.reshape(1, MIX_W))
    return out.reshape(bsz * seq, MIX_W)


def _rope_tables(seq):
    inv_freq = ROPE_THETA ** (-jnp.arange(0, DIFF_DH, 2, dtype=F32) / DIFF_DH)
    ang = jnp.arange(seq, dtype=F32)[:, None] * inv_freq[None, :]
    cos, sin = jnp.cos(ang), jnp.sin(ang)
    reps = LANES // DIFF_DH
    cos_t = jnp.tile(jnp.concatenate([cos, cos], axis=-1), (1, reps))
    sin_t = jnp.tile(jnp.concatenate([-sin, sin], axis=-1), (1, reps))
    return cos_t, sin_t


def kernel(x, ffn1_pre_g, ffn1_w_gate, ffn1_w_up, ffn1_w_down, ffn1_post_g, mix_pre_g, w_in, gdn_conv_w, gdn_a_log, gdn_dt_bias, gdn_norm_g, diff_lam_q1, diff_lam_k1, diff_lam_q2, diff_lam_k2, diff_norm_g, ssd_conv_w, ssd_conv_b, ssd_a_log, ssd_dt_bias, ssd_d, ssd_norm_g, mlstm_i_bias, mlstm_f_bias, mlstm_norm_g, w_out, mix_post_g, ffn2_pre_g, ffn2_w_gate, ffn2_w_up, ffn2_w_down, ffn2_post_g):
    bsz, seq, d = x.shape
    x2 = x.reshape(bsz * seq, d)
    cos_t, sin_t = _rope_tables(seq)
    for l in range(DEPTH):
        lambda_init = 0.8 - 0.6 * math.exp(-0.3 * l)
        x2 = _ffn_call(x2, ffn1_pre_g[l], ffn1_w_gate[l], ffn1_w_up[l], ffn1_w_down[l], ffn1_post_g[l])
        gdn_slab, ssd_slab, ml_slab, dq, dk, dv = _proj_call(x2, mix_pre_g[l], w_in[l], cos_t, sin_t, seq)
        out_a = _gdn_call(gdn_slab, gdn_conv_w[l], gdn_a_log[l], gdn_dt_bias[l], gdn_norm_g[l], bsz, seq)
        lam_p = jnp.stack([diff_lam_q1[l], diff_lam_k1[l], diff_lam_q2[l], diff_lam_k2[l]])
        out_b = _attn_call(dq, dk, dv, lam_p, diff_norm_g[l], lambda_init, bsz, seq)
        out_c = _ssd_call(ssd_slab, ssd_conv_w[l], ssd_conv_b[l], ssd_a_log[l], ssd_dt_bias[l], ssd_d[l],
                          ssd_norm_g[l], bsz, seq)
        out_d = _mlstm_call(ml_slab, mlstm_i_bias[l], mlstm_f_bias[l], mlstm_norm_g[l], bsz, seq)
        x2 = _outproj_call(x2, (out_a, out_b, out_c, out_d), w_out[l], mix_post_g[l])
        x2 = _ffn_call(x2, ffn2_pre_g[l], ffn2_w_gate[l], ffn2_w_up[l], ffn2_w_down[l], ffn2_post_g[l])
    return x2.reshape(bsz, seq, d)
```

```python
import functools
import math

import jax
import jax.numpy as jnp
from jax import lax
from jax.experimental import pallas as pl
from jax.experimental.pallas import tpu as pltpu

F32 = jnp.float32
MXU_DTYPE = jnp.bfloat16

D_MODEL = 1024
D_FF = 2816
DEPTH = 2
CHUNK = 64
CHUNK_SHIFT = 6
CONV_K = 4
EPS = 1e-6
ROPE_THETA = 10000.0
HEADS = 4
HEAD_W = 64
MIX_W = HEADS * HEAD_W
DIFF_DH = 32
DIFF_DH_SHIFT = 5
LOG2_E = math.log2(math.e)
SSD_N = 128
LANES = 128
NEG = -1e30

_GDN_COLS = (0, 1032)
_DIFF_COLS = (1032, 1800)
_SSD_COLS = (1800, 2828)
_MLSTM_COLS = (2828, 3860)
SLAB_W = 1152

FFN_TM = 512
FFN_FC = 256
MIX_T = 256
ATT_TQ = 128
ATT_RB = 32
ATT_TK = 512
VMEM_LIMIT = 56 * 1024 * 1024


def _dot(a, b):
    return jnp.dot(a.astype(MXU_DTYPE), b.astype(MXU_DTYPE), preferred_element_type=F32)


def _dot_nt(a, b):
    return lax.dot_general(a.astype(MXU_DTYPE), b.astype(MXU_DTYPE), (((1,), (1,)), ((), ())),
                           preferred_element_type=F32)


def _dot_tn(a, b):
    return lax.dot_general(a.astype(MXU_DTYPE), b.astype(MXU_DTYPE), (((0,), (0,)), ((), ())),
                           preferred_element_type=F32)


def _split3(x):
    x1 = x.astype(MXU_DTYPE)
    r1 = x - x1.astype(F32)
    x2 = r1.astype(MXU_DTYPE)
    return x1, x2, (r1 - x2.astype(F32)).astype(MXU_DTYPE)


def _dot_sel(sel, x, terms=3):
    sel = sel.astype(MXU_DTYPE)
    return sum(jnp.dot(sel, xi, preferred_element_type=F32) for xi in _split3(x)[:terms])


def _dot_xsel(x, sel, terms=3):
    sel = sel.astype(MXU_DTYPE)
    return sum(jnp.dot(xi, sel, preferred_element_type=F32) for xi in _split3(x)[:terms])


def _chunk_last(x):
    t, w = x.shape
    return jnp.concatenate([jnp.broadcast_to(x[c + CHUNK - 1:c + CHUNK, :], (CHUNK, w))
                            for c in range(0, t, CHUNK)], axis=0)


def _rms(x, g):
    return x * lax.rsqrt(jnp.mean(x * x, axis=-1, keepdims=True) + EPS) * g


def _silu(x):
    return x / (1.0 + jnp.exp(-x))


def _sigmoid(x):
    return 1.0 / (1.0 + jnp.exp(-x))


def _softplus(x):
    return jnp.maximum(x, 0.0) + jnp.log(1.0 + jnp.exp(-jnp.abs(x)))


def _iota(shape, dim):
    return lax.broadcasted_iota(jnp.int32, shape, dim)


def _chunk_consts(t):
    r = _iota((t, t), 0)
    c = _iota((t, t), 1)
    same = (r >> CHUNK_SHIFT) == (c >> CHUNK_SHIFT)
    ltile = jnp.where(same & (c <= r), 1.0, 0.0).astype(F32)
    csame = jnp.where(same, 1.0, 0.0).astype(F32)
    pos = _iota((t, MIX_W), 0) & (CHUNK - 1)
    m = _iota((t, MIX_W), 1) & (CHUNK - 1)
    return ltile, csame, pos, m


def _head_expand(first_lane):
    r = _iota((LANES, MIX_W), 0)
    c = _iota((LANES, MIX_W), 1)
    return jnp.where(r == (c >> CHUNK_SHIFT) + first_lane, 1.0, 0.0).astype(F32)


def _head_ones():
    r = _iota((MIX_W, MIX_W), 0)
    c = _iota((MIX_W, MIX_W), 1)
    return jnp.where((r >> CHUNK_SHIFT) == (c >> CHUNK_SHIFT), 1.0, 0.0).astype(F32)


def _causal_conv(u, w_ref, carry_ref, buf_ref, t):
    buf_ref[0:8, :] = carry_ref[...]
    buf_ref[8:8 + t, :] = u
    carry_ref[...] = u[t - 8:t, :]
    acc = w_ref[CONV_K - 1:CONV_K, :] * u
    for j in range(CONV_K - 1):
        off = 8 - (CONV_K - 1) + j
        acc = acc + w_ref[j:j + 1, :] * buf_ref[off:off + t, :]
    return acc


def _ffn_kernel(x_ref, pre_ref, wgu_ref, wd_ref, post_ref, o_ref, h_ref, acc_ref, *, n_chunks, fc):
    x = x_ref[...]
    h_ref[...] = _rms(x, pre_ref[...]).astype(h_ref.dtype)
    acc_ref[...] = jnp.zeros_like(acc_ref)

    def body(c, carry):
        r = jnp.dot(h_ref[...], wgu_ref[c], preferred_element_type=F32)
        a = _silu(r[:, :fc]) * r[:, fc:]
        acc_ref[...] += jnp.dot(a.astype(wd_ref.dtype), wd_ref[c], preferred_element_type=F32)
        return carry

    lax.fori_loop(0, n_chunks, body, 0)
    o_ref[...] = x + 0.5 * _rms(acc_ref[...], post_ref[...])


def _ffn_call(x2, pre_g, w_gate, w_up, w_down, post_g, tm=FFN_TM, fc=FFN_FC):
    m, d = x2.shape
    ff = w_gate.shape[1]
    n_chunks = ff // fc
    wg = w_gate.reshape(d, n_chunks, fc).transpose(1, 0, 2)
    wu = w_up.reshape(d, n_chunks, fc).transpose(1, 0, 2)
    wgu = jnp.concatenate([wg, wu], axis=-1).astype(MXU_DTYPE)
    wd = w_down.reshape(n_chunks, fc, d).astype(MXU_DTYPE)
    const3 = lambda i: (0, 0, 0)
    return pl.pallas_call(
        functools.partial(_ffn_kernel, n_chunks=n_chunks, fc=fc),
        out_shape=jax.ShapeDtypeStruct((m, d), F32),
        grid=(m // tm,),
        in_specs=[
            pl.BlockSpec((tm, d), lambda i: (i, 0)),
            pl.BlockSpec((1, d), lambda i: (0, 0)),
            pl.BlockSpec((n_chunks, d, 2 * fc), const3),
            pl.BlockSpec((n_chunks, fc, d), const3),
            pl.BlockSpec((1, d), lambda i: (0, 0)),
        ],
        out_specs=pl.BlockSpec((tm, d), lambda i: (i, 0)),
        scratch_shapes=[pltpu.VMEM((tm, d), MXU_DTYPE), pltpu.VMEM((tm, d), F32)],
        compiler_params=pltpu.CompilerParams(dimension_semantics=("parallel",), vmem_limit_bytes=VMEM_LIMIT),
        name="ffn",
    )(x2, pre_g.reshape(1, d), wgu, wd, post_g.reshape(1, d))


def _rope_half(x, cos, sin):
    lane = _iota(x.shape, 1)
    half = DIFF_DH // 2
    partner = jnp.where((lane & (DIFF_DH - 1)) < half, pltpu.roll(x, LANES - half, 1), pltpu.roll(x, half, 1))
    return x * cos + partner * sin


def _proj_kernel(x_ref, g_ref, wa_ref, wb_ref, wc_ref, wd_ref, cos_ref, sin_ref,
                 gdn_ref, ssd_ref, ml_ref, dq_ref, dk_ref, dv_ref):
    h = _rms(x_ref[...], g_ref[...]).astype(wa_ref.dtype)
    gdn_ref[...] = jnp.dot(h, wa_ref[...], preferred_element_type=F32)
    ssd_ref[...] = jnp.dot(h, wc_ref[...], preferred_element_type=F32)
    ml_ref[...] = jnp.dot(h, wd_ref[...], preferred_element_type=F32)
    d = jnp.dot(h, wb_ref[...], preferred_element_type=F32)
    cos = cos_ref[...]
    sin = sin_ref[...]
    scale = DIFF_DH ** -0.5 * LOG2_E
    for half in range(2):
        lo = half * LANES
        dq_ref[:, lo:lo + LANES] = (_rope_half(d[:, lo:lo + LANES], cos, sin) * scale).astype(dq_ref.dtype)
        dk_ref[:, lo:lo + LANES] = _rope_half(d[:, MIX_W + lo:MIX_W + lo + LANES], cos, sin).astype(dk_ref.dtype)
    ones = jnp.ones((d.shape[0], HEAD_W), F32)
    for h in range(HEADS):
        v_h = d[:, 2 * MIX_W + h * HEAD_W:2 * MIX_W + (h + 1) * HEAD_W]
        dv_ref[:, h * LANES:(h + 1) * LANES] = jnp.concatenate([v_h, ones], axis=-1).astype(dv_ref.dtype)


def _pad_cols(w, width):
    return jnp.pad(w, ((0, 0), (0, width - w.shape[1])))


def _proj_call(x2, pre_g, w_in, cos_t, sin_t, seq, tm=FFN_TM):
    m, d = x2.shape
    wa = _pad_cols(w_in[:, _GDN_COLS[0]:_GDN_COLS[1]], SLAB_W).astype(MXU_DTYPE)
    wb = w_in[:, _DIFF_COLS[0]:_DIFF_COLS[1]].astype(MXU_DTYPE)
    wc = _pad_cols(w_in[:, _SSD_COLS[0]:_SSD_COLS[1]], SLAB_W).astype(MXU_DTYPE)
    wd = _pad_cols(w_in[:, _MLSTM_COLS[0]:_MLSTM_COLS[1]], SLAB_W).astype(MXU_DTYPE)
    tiles_per_seq = seq // tm
    full = lambda i: (0, 0)
    row = lambda i: (i, 0)
    return pl.pallas_call(
        _proj_kernel,
        out_shape=(
            jax.ShapeDtypeStruct((m, SLAB_W), F32),
            jax.ShapeDtypeStruct((m, SLAB_W), F32),
            jax.ShapeDtypeStruct((m, SLAB_W), F32),
            jax.ShapeDtypeStruct((m, MIX_W), MXU_DTYPE),
            jax.ShapeDtypeStruct((m, MIX_W), MXU_DTYPE),
            jax.ShapeDtypeStruct((m, HEADS * LANES), MXU_DTYPE),
        ),
        grid=(m // tm,),
        in_specs=[
            pl.BlockSpec((tm, d), row),
            pl.BlockSpec((1, d), full),
            pl.BlockSpec((d, SLAB_W), full),
            pl.BlockSpec((d, 3 * MIX_W), full),
            pl.BlockSpec((d, SLAB_W), full),
            pl.BlockSpec((d, SLAB_W), full),
            pl.BlockSpec((tm, LANES), lambda i: (i % tiles_per_seq, 0)),
            pl.BlockSpec((tm, LANES), lambda i: (i % tiles_per_seq, 0)),
        ],
        out_specs=(
            pl.BlockSpec((tm, SLAB_W), row),
            pl.BlockSpec((tm, SLAB_W), row),
            pl.BlockSpec((tm, SLAB_W), row),
            pl.BlockSpec((tm, MIX_W), row),
            pl.BlockSpec((tm, MIX_W), row),
            pl.BlockSpec((tm, HEADS * LANES), row),
        ),
        compiler_params=pltpu.CompilerParams(dimension_semantics=("parallel",), vmem_limit_bytes=VMEM_LIMIT),
        name="mixer_in_proj",
    )(x2, pre_g.reshape(1, d), wa, wb, wc, wd, cos_t, sin_t)


def _outproj_kernel(x_ref, a_ref, b_ref, c_ref, d_ref, w_ref, g_ref, o_ref):
    mixed = jnp.concatenate([a_ref[...], b_ref[...], c_ref[...], d_ref[...]], axis=-1).astype(w_ref.dtype)
    y = jnp.dot(mixed, w_ref[...], preferred_element_type=F32)
    o_ref[...] = x_ref[...] + _rms(y, g_ref[...])


def _outproj_call(x2, outs, w_out, post_g, tm=FFN_TM):
    m, d = x2.shape
    row = lambda i: (i, 0)
    full = lambda i: (0, 0)
    return pl.pallas_call(
        _outproj_kernel,
        out_shape=jax.ShapeDtypeStruct((m, d), F32),
        grid=(m // tm,),
        in_specs=[pl.BlockSpec((tm, d), row)] + [pl.BlockSpec((tm, MIX_W), row)] * 4
        + [pl.BlockSpec((d, d), full), pl.BlockSpec((1, d), full)],
        out_specs=pl.BlockSpec((tm, d), row),
        compiler_params=pltpu.CompilerParams(dimension_semantics=("parallel",), vmem_limit_bytes=VMEM_LIMIT),
        name="mixer_out_proj",
    )(x2, *outs, w_out.astype(MXU_DTYPE), post_g.reshape(1, d))


def _attn_kernel(q_ref, k_ref, v_ref, lam_ref, g_ref, o_ref, qs_ref, m_ref, acc_ref, p_ref, s_ref, alpha_ref,
                 *, tq, tk, lambda_init):
    i = pl.program_id(1)
    n_maps = 2 * HEADS
    q = q_ref[0]
    lane = _iota((tq, MIX_W), 1)
    for mp in range(n_maps):
        qs_ref[mp] = jnp.where((lane >> DIFF_DH_SHIFT) == mp, q, jnp.zeros_like(q))
    m_ref[...] = jnp.full_like(m_ref, NEG)
    acc_ref[...] = jnp.zeros_like(acc_ref)
    qpos = i * tq + _iota((tq, 1), 0)
    limit = ((qpos >> CHUNK_SHIFT) + 1) << CHUNK_SHIFT
    n_kt = (i * tq + tq + tk - 1) // tk

    def step(j, masked):
        start = pl.multiple_of(j * tk, tk)
        kt = k_ref[0, pl.ds(start, tk), :]
        if masked:
            key = start + _iota((1, tk), 1)
        def scores(h):
            s = lax.dot_general(qs_ref[2 * h:2 * h + 2].reshape(2 * tq, MIX_W), kt, (((1,), (1,)), ((), ())),
                                preferred_element_type=F32)
            s_ref[2 * h:2 * h + 2] = s.reshape(2, tq, tk)

        def softmax(h):
            for mp in (2 * h, 2 * h + 1):
                for r0 in range(0, tq, ATT_RB):
                    rows = slice(r0, r0 + ATT_RB)
                    s = s_ref[mp, rows, :]
                    if masked:
                        s = jnp.where(key < limit[rows], s, NEG)
                    m_old = m_ref[mp, rows, :]
                    m_new = jnp.maximum(m_old, jnp.max(s, axis=-1, keepdims=True))
                    p_ref[mp, rows, :] = jnp.exp2(s - m_new).astype(p_ref.dtype)
                    alpha_ref[mp, rows, :] = jnp.exp2(m_old - m_new)
                    m_ref[mp, rows, :] = m_new

        def values(h):
            pv = jnp.dot(p_ref[2 * h:2 * h + 2].reshape(2 * tq, tk),
                         v_ref[0, pl.ds(start, tk), h * LANES:(h + 1) * LANES], preferred_element_type=F32)
            acc_ref[2 * h:2 * h + 2] = alpha_ref[2 * h:2 * h + 2] * acc_ref[2 * h:2 * h + 2] + pv.reshape(2, tq, LANES)

        scores(0)
        scores(1)
        for h in range(HEADS):
            softmax(h)
            if h + 2 < HEADS:
                scores(h + 2)
            values(h)

    def body(j, carry):
        step(j, False)
        return carry

    lax.fori_loop(0, n_kt - 1, body, 0)
    step(n_kt - 1, True)

    lam_p = lam_ref[...]
    lam = (jnp.exp(jnp.sum(lam_p[0:1] * lam_p[1:2], axis=-1, keepdims=True))
           - jnp.exp(jnp.sum(lam_p[2:3] * lam_p[3:4], axis=-1, keepdims=True)) + lambda_init)
    outs = []
    for h in range(HEADS):
        a1 = acc_ref[2 * h]
        a2 = acc_ref[2 * h + 1]
        o = a1[:, :HEAD_W] / a1[:, HEAD_W:HEAD_W + 1] - lam * (a2[:, :HEAD_W] / a2[:, HEAD_W:HEAD_W + 1])
        o = o * lax.rsqrt(jnp.mean(o * o, axis=-1, keepdims=True) + EPS)
        outs.append(o)
    o_ref[0] = jnp.concatenate(outs, axis=-1) * g_ref[...] * (1.0 - lambda_init)


def _attn_call(dq, dk, dv, lam_p, norm_g, lambda_init, bsz, seq, tq=ATT_TQ, tk=ATT_TK):
    q3 = dq.reshape(bsz, seq, MIX_W)
    k3 = dk.reshape(bsz, seq, MIX_W)
    v3 = dv.reshape(bsz, seq, HEADS * LANES)
    n_maps = 2 * HEADS
    out = pl.pallas_call(
        functools.partial(_attn_kernel, tq=tq, tk=tk, lambda_init=lambda_init),
        out_shape=jax.ShapeDtypeStruct((bsz, seq, MIX_W), F32),
        grid=(bsz, seq // tq),
        in_specs=[
            pl.BlockSpec((1, tq, MIX_W), lambda b, i: (b, i, 0)),
            pl.BlockSpec((1, seq, MIX_W), lambda b, i: (b, 0, 0)),
            pl.BlockSpec((1, seq, HEADS * LANES), lambda b, i: (b, 0, 0)),
            pl.BlockSpec((4, DIFF_DH), lambda b, i: (0, 0)),
            pl.BlockSpec((1, MIX_W), lambda b, i: (0, 0)),
        ],
        out_specs=pl.BlockSpec((1, tq, MIX_W), lambda b, i: (b, i, 0)),
        scratch_shapes=[
            pltpu.VMEM((n_maps, tq, MIX_W), MXU_DTYPE),
            pltpu.VMEM((n_maps, tq, 1), F32),
            pltpu.VMEM((n_maps, tq, LANES), F32),
            pltpu.VMEM((n_maps, tq, tk), MXU_DTYPE),
            pltpu.VMEM((n_maps, tq, tk), F32),
            pltpu.VMEM((n_maps, tq, 1), F32),
        ],
        compiler_params=pltpu.CompilerParams(dimension_semantics=("parallel", "arbitrary"),
                                             vmem_limit_bytes=VMEM_LIMIT),
        name="diff_attention",
    )(q3, k3, v3, lam_p, jnp.tile(norm_g, HEADS).reshape(1, MIX_W))
    return out.reshape(bsz * seq, MIX_W)


def _ssd_kernel(slab_ref, cw_ref, cb_ref, alog_ref, dtb_ref, dskip_ref, g_ref, y_ref,
                carry_ref, buf_ref, st_ref, xc_ref, xcd_ref, bm_ref, cm_ref, lm_ref, ea_ref, el_ref, yo_ref, *, t):
    @pl.when(pl.program_id(1) == 0)
    def _():
        carry_ref[...] = jnp.zeros_like(carry_ref)
        st_ref[...] = jnp.zeros_like(st_ref)

    ltile, _, pos, mm = _chunk_consts(t)
    expand = _head_expand(0)
    z = slab_ref[0, :, 0:MIX_W]
    xbc = _silu(_causal_conv(slab_ref[0, :, MIX_W:4 * MIX_W], cw_ref, carry_ref, buf_ref, t) + cb_ref[...])
    x = xbc[:, 0:MIX_W]
    bm_ref[...] = xbc[:, MIX_W:2 * MIX_W].astype(bm_ref.dtype)
    cm_ref[...] = xbc[:, 2 * MIX_W:3 * MIX_W].astype(cm_ref.dtype)
    dt = _softplus(slab_ref[0, :, 4 * MIX_W:] + dtb_ref[...])
    da = dt * (-jnp.exp(alog_ref[...]))
    dt_e = _dot_xsel(dt, expand)
    da_e = _dot_xsel(da, expand)
    acs_e = _dot_sel(ltile,da_e)
    acs_last_e = _chunk_last(acs_e)
    ldiff = _dot_sel(ltile,jnp.where(pos > mm, da_e, 0.0))
    lm_ref[...] = jnp.where(mm <= pos, jnp.exp(ldiff), 0.0)
    xc = x * dt_e
    xc_ref[...] = xc.astype(xc_ref.dtype)
    xcd_ref[...] = (xc * jnp.exp(acs_last_e - acs_e)).astype(xcd_ref.dtype)
    ea_ref[...] = jnp.exp(acs_e)
    el_ref[...] = jnp.exp(acs_last_e)

    for c in range(t // CHUNK):
        r0 = c * CHUNK
        rows = slice(r0, r0 + CHUNK)
        for grp in range(2):
            gl = slice(grp * SSD_N, (grp + 1) * SSD_N)
            bg = bm_ref[rows, gl]
            cg = cm_ref[rows, gl]
            cb = _dot_nt(cg, bg)
            ys = []
            for hh in range(2):
                hl = slice(grp * SSD_N + hh * HEAD_W, grp * SSD_N + (hh + 1) * HEAD_W)
                ys.append(_dot(cb * lm_ref[rows, hl], xc_ref[rows, hl]))
            y_diag = jnp.concatenate(ys, axis=-1)
            st = st_ref[grp]
            y_off = _dot(cg, st) * ea_ref[rows, gl]
            st_ref[grp] = st * el_ref[r0:r0 + 1, gl] + _dot_tn(bg, xcd_ref[rows, gl])
            yo_ref[rows, gl] = y_diag + y_off

    y = (yo_ref[...] + x * dskip_ref[...]) * _silu(z)
    gs = g_ref[...]
    for grp in range(2):
        gl = slice(grp * SSD_N, (grp + 1) * SSD_N)
        y_ref[0, :, gl] = _rms(y[:, gl], gs[:, gl])


def _ssd_call(slab, conv_w, conv_b, a_log, dt_bias, d_skip, norm_g, bsz, seq, t=MIX_T):
    ch = 3 * MIX_W
    lane_pad = lambda v: jnp.pad(v, (0, LANES - v.shape[0])).reshape(1, LANES)
    full = lambda b, i: (0, 0)
    out = pl.pallas_call(
        functools.partial(_ssd_kernel, t=t),
        out_shape=jax.ShapeDtypeStruct((bsz, seq, MIX_W), F32),
        grid=(bsz, seq // t),
        in_specs=[
            pl.BlockSpec((1, t, SLAB_W), lambda b, i: (b, i, 0)),
            pl.BlockSpec((CONV_K, ch), full),
            pl.BlockSpec((1, ch), full),
            pl.BlockSpec((1, LANES), full),
            pl.BlockSpec((1, LANES), full),
            pl.BlockSpec((1, MIX_W), full),
            pl.BlockSpec((1, MIX_W), full),
        ],
        out_specs=pl.BlockSpec((1, t, MIX_W), lambda b, i: (b, i, 0)),
        scratch_shapes=[
            pltpu.VMEM((8, ch), F32),
            pltpu.VMEM((t + 8, ch), F32),
            pltpu.VMEM((2, SSD_N, SSD_N), F32),
            pltpu.VMEM((t, MIX_W), MXU_DTYPE),
            pltpu.VMEM((t, MIX_W), MXU_DTYPE),
            pltpu.VMEM((t, MIX_W), MXU_DTYPE),
            pltpu.VMEM((t, MIX_W), MXU_DTYPE),
            pltpu.VMEM((t, MIX_W), F32),
            pltpu.VMEM((t, MIX_W), F32),
            pltpu.VMEM((t, MIX_W), F32),
            pltpu.VMEM((t, MIX_W), F32),
        ],
        compiler_params=pltpu.CompilerParams(dimension_semantics=("parallel", "arbitrary"),
                                             vmem_limit_bytes=VMEM_LIMIT),
        name="ssd_mixer",
    )(slab.reshape(bsz, seq, SLAB_W), conv_w, conv_b.reshape(1, ch), lane_pad(a_log), lane_pad(dt_bias),
      jnp.repeat(d_skip, HEAD_W).reshape(1, MIX_W), norm_g.reshape(1, MIX_W))
    return out.reshape(bsz * seq, MIX_W)


def _bd(x, mask):
    x4 = jnp.concatenate([x] * HEADS, axis=0)
    return jnp.where(mask, x4, jnp.zeros_like(x4))


def _split(x):
    hi = x.astype(MXU_DTYPE)
    return hi, (x - hi.astype(F32)).astype(MXU_DTYPE)


def _bd_matmul3(a, b, mask):
    ah, al = _split(a)
    bh, bl = _split(b)
    m = a.shape[0]
    r = jnp.dot(jnp.concatenate([ah, al], axis=0), _bd(bh, mask), preferred_element_type=F32)
    return r[:m] + r[m:] + jnp.dot(ah, _bd(bl, mask), preferred_element_type=F32)


def _gdn_kernel(slab_ref, cw_ref, alog_ref, dtb_ref, g_ref, y_ref,
                carry_ref, buf_ref, st_ref, qn_ref, kn_ref, qd_ref, kd_ref, kb_ref, vb_ref, kbg_ref, dec_ref,
                egl_ref, u_ref, w_ref, attn_ref, o_ref, *, t):
    @pl.when(pl.program_id(1) == 0)
    def _():
        carry_ref[...] = jnp.zeros_like(carry_ref)
        st_ref[...] = jnp.zeros_like(st_ref)

    ltile, _, pos, mm = _chunk_consts(t)
    ones_h = _head_ones()
    qkv = _silu(_causal_conv(slab_ref[0, :, 0:3 * MIX_W], cw_ref, carry_ref, buf_ref, t))
    q = qkv[:, 0:MIX_W]
    k = qkv[:, MIX_W:2 * MIX_W]
    v = qkv[:, 2 * MIX_W:]
    qn = q * lax.rsqrt(_dot_xsel(q * q, ones_h, terms=2) + EPS) * (HEAD_W ** -0.5)
    kn = k * lax.rsqrt(_dot_xsel(k * k, ones_h, terms=2) + EPS)
    ba = slab_ref[0, :, 4 * MIX_W:]
    beta_e = _dot_xsel(_sigmoid(ba), _head_expand(0))
    g4 = -jnp.exp(alog_ref[...]) * _softplus(ba + dtb_ref[...])
    g_e = _dot_xsel(g4, _head_expand(HEADS))
    gc_e = _dot_sel(ltile,g_e)
    gl_e = _chunk_last(gc_e)
    diff = _dot_sel(ltile,jnp.where(pos > mm, g_e, 0.0))
    dec_ref[...] = jnp.where(mm <= pos, jnp.exp(diff), 0.0)
    egc = jnp.exp(gc_e)
    kb = kn * beta_e
    qn_ref[...] = qn
    kn_ref[...] = kn
    qd_ref[...] = qn * egc
    kd_ref[...] = kn * jnp.exp(gl_e - gc_e)
    kb_ref[...] = kb
    vb_ref[...] = v * beta_e
    kbg_ref[...] = kb * egc
    egl_ref[...] = jnp.exp(gl_e)

    head_mask = (_iota((MIX_W, MIX_W), 0) >> CHUNK_SHIFT) == (_iota((MIX_W, MIX_W), 1) >> CHUNK_SHIFT)
    r_ss = _iota((CHUNK, MIX_W), 0)
    m_ss = _iota((CHUNK, MIX_W), 1) & (CHUNK - 1)
    eye_ss = jnp.where(r_ss == m_ss, 1.0, 0.0).astype(F32)
    n_chunks = t // CHUNK
    chunk_rows = [slice(c * CHUNK, (c + 1) * CHUNK) for c in range(n_chunks)]

    ys, tms = [], []
    for rows in chunk_rows:
        lhs = jnp.concatenate([kb_ref[rows, :], qn_ref[rows, :]], axis=0).astype(MXU_DTYPE)
        raw = lax.dot_general(lhs, _bd(kn_ref[rows, :].astype(MXU_DTYPE), head_mask), (((1,), (1,)), ((), ())),
                              preferred_element_type=F32)
        dec = dec_ref[rows, :]
        attn_ref[rows, :] = raw[CHUNK:] * dec
        y = jnp.where(m_ss < r_ss, -(raw[:CHUNK] * dec), 0.0)
        ys.append(y)
        tms.append(eye_ss + y)
    ys = [_bd_matmul3(y, y, head_mask) for y in ys]
    for _ in range(4):
        rs = [_bd_matmul3(jnp.concatenate([tm, y], axis=0), y, head_mask) for tm, y in zip(tms, ys)]
        tms = [tm + r[:CHUNK] for tm, r in zip(tms, rs)]
        ys = [r[CHUNK:] for r in rs]
    tms = [tm + _bd_matmul3(tm, y, head_mask) for tm, y in zip(tms, ys)]
    for rows, tm in zip(chunk_rows, tms):
        rhs = jnp.concatenate([_bd(vb_ref[rows, :].astype(MXU_DTYPE), head_mask),
                               _bd(kbg_ref[rows, :].astype(MXU_DTYPE), head_mask)], axis=1)
        uw = jnp.dot(tm.astype(MXU_DTYPE), rhs, preferred_element_type=F32)
        u_ref[rows, :] = uw[:, :MIX_W]
        w_ref[rows, :] = uw[:, MIX_W:]

    st = st_ref[...]
    for c, rows in enumerate(chunk_rows):
        lhs = jnp.concatenate([w_ref[rows, :], qd_ref[rows, :]], axis=0).astype(MXU_DTYPE)
        r = jnp.dot(lhs, st.astype(MXU_DTYPE), preferred_element_type=F32)
        v_new = u_ref[rows, :] - r[:CHUNK]
        v_new_c = v_new.astype(MXU_DTYPE)
        o_ref[rows, :] = r[CHUNK:] + jnp.dot(attn_ref[rows, :].astype(MXU_DTYPE), _bd(v_new_c, head_mask),
                                             preferred_element_type=F32)
        kv = lax.dot_general(kd_ref[rows, :].astype(MXU_DTYPE), v_new_c, (((0,), (0,)), ((), ())),
                             preferred_element_type=F32)
        st = st * egl_ref[c * CHUNK:c * CHUNK + 1, :] + jnp.where(head_mask, kv, 0.0)
    st_ref[...] = st

    o = o_ref[...]
    o = o * lax.rsqrt(_dot_xsel(o * o, ones_h, terms=2) * (1.0 / HEAD_W) + EPS) * g_ref[...]
    y_ref[0] = o * _silu(slab_ref[0, :, 3 * MIX_W:4 * MIX_W])


def _gdn_call(slab, conv_w, a_log, dt_bias, norm_g, bsz, seq, t=MIX_T):
    ch = 3 * MIX_W
    gate_pad = lambda v: jnp.pad(v, (HEADS, LANES - 2 * HEADS)).reshape(1, LANES)
    full = lambda b, i: (0, 0)
    tile = lambda: pltpu.VMEM((t, MIX_W), F32)
    out = pl.pallas_call(
        functools.partial(_gdn_kernel, t=t),
        out_shape=jax.ShapeDtypeStruct((bsz, seq, MIX_W), F32),
        grid=(bsz, seq // t),
        in_specs=[
            pl.BlockSpec((1, t, SLAB_W), lambda b, i: (b, i, 0)),
            pl.BlockSpec((CONV_K, ch), full),
            pl.BlockSpec((1, LANES), full),
            pl.BlockSpec((1, LANES), full),
            pl.BlockSpec((1, MIX_W), full),
        ],
        out_specs=pl.BlockSpec((1, t, MIX_W), lambda b, i: (b, i, 0)),
        scratch_shapes=[
            pltpu.VMEM((8, ch), F32),
            pltpu.VMEM((t + 8, ch), F32),
            pltpu.VMEM((MIX_W, MIX_W), F32),
            tile(), tile(), tile(), tile(), tile(), tile(), tile(), tile(), tile(), tile(), tile(), tile(), tile(),
        ],
        compiler_params=pltpu.CompilerParams(dimension_semantics=("parallel", "arbitrary"),
                                             vmem_limit_bytes=VMEM_LIMIT),
        name="gated_deltanet",
    )(slab.reshape(bsz, seq, SLAB_W), conv_w, gate_pad(a_log), gate_pad(dt_bias),
      jnp.tile(norm_g, HEADS).reshape(1, MIX_W))
    return out.reshape(bsz * seq, MIX_W)


def _mlstm_kernel(slab_ref, bias_ref, g_ref, y_ref, c_ref, m_ref, dl_ref, b_ref, bl_ref, ge_ref, o_ref, *, t):
    @pl.when(pl.program_id(1) == 0)
    def _():
        c_ref[...] = jnp.zeros_like(c_ref)
        m_ref[...] = jnp.zeros_like(m_ref)

    ltile, csame, pos, mm = _chunk_consts(t)
    gates = slab_ref[0, :, 4 * MIX_W:] + bias_ref[...]
    li_e = _dot_xsel(gates, _head_expand(0))
    lf_e = _dot_xsel(-_softplus(-gates), _head_expand(HEADS))
    b_e = _dot_sel(ltile,lf_e)
    dlog = _dot_sel(ltile,jnp.where(pos > mm, lf_e, 0.0)) + _dot_sel(csame,jnp.where(pos == mm, li_e, 0.0))
    dl_ref[...] = jnp.where(mm <= pos, dlog, NEG)
    bl_e = _chunk_last(b_e)
    b_ref[...] = b_e
    bl_ref[...] = bl_e
    ge_ref[...] = bl_e - b_e + li_e

    ones_col = jnp.where(_iota((CHUNK, HEAD_W), 1) == 0, 1.0, 0.0).astype(F32)
    for c in range(t // CHUNK):
        r0 = c * CHUNK
        rows = slice(r0, r0 + CHUNK)
        for h in range(HEADS):
            hl = slice(h * HEAD_W, (h + 1) * HEAD_W)
            q_h = slab_ref[0, rows, hl]
            k_h = slab_ref[0, rows, MIX_W + h * HEAD_W:MIX_W + (h + 1) * HEAD_W] * (HEAD_W ** -0.5)
            v_aug = jnp.concatenate([slab_ref[0, rows, 2 * MIX_W + h * HEAD_W:2 * MIX_W + (h + 1) * HEAD_W],
                                     ones_col], axis=-1)
            m_st = m_ref[h][0:1, 0:1]
            c_aug = c_ref[h]
            bcol = b_ref[rows, h * HEAD_W:h * HEAD_W + 1]
            dl = dl_ref[rows, hl]
            m_t = jnp.maximum(bcol + m_st, jnp.max(dl, axis=-1, keepdims=True))
            w_st = jnp.exp(bcol + m_st - m_t)
            s = _dot_nt(q_h, k_h) * jnp.exp(dl - m_t)
            nd = w_st * _dot(q_h, c_aug) + _dot(s, v_aug)
            den = nd[:, HEAD_W:HEAD_W + 1]
            o_ref[rows, hl] = nd[:, :HEAD_W] / jnp.maximum(jnp.abs(den), jnp.exp(-m_t))
            ge = ge_ref[rows, h * HEAD_W:h * HEAD_W + 1]
            blast = bl_ref[r0:r0 + 1, h * HEAD_W:h * HEAD_W + 1]
            m_new = jnp.maximum(blast + m_st, jnp.max(ge, axis=0, keepdims=True))
            w_old = jnp.exp(blast + m_st - m_new)
            k_w = k_h * jnp.exp(ge - m_new)
            c_ref[h] = c_aug * w_old + _dot_tn(k_w, v_aug)
            m_ref[h] = jnp.broadcast_to(m_new, (8, LANES))

    hcat = o_ref[...] * _sigmoid(slab_ref[0, :, 3 * MIX_W:4 * MIX_W])
    ms = _dot_xsel(hcat * hcat, _head_ones(), terms=2) * (1.0 / HEAD_W)
    y_ref[0] = hcat * lax.rsqrt(ms + EPS) * g_ref[...]


def _mlstm_call(slab, i_bias, f_bias, norm_g, bsz, seq, t=MIX_T):
    bias = jnp.pad(jnp.concatenate([i_bias, f_bias]), (0, LANES - 2 * HEADS)).reshape(1, LANES)
    full = lambda b, i: (0, 0)
    tile = lambda: pltpu.VMEM((t, MIX_W), F32)
    out = pl.pallas_call(
        functools.partial(_mlstm_kernel, t=t),
        out_shape=jax.ShapeDtypeStruct((bsz, seq, MIX_W), F32),
        grid=(bsz, seq // t),
        in_specs=[
            pl.BlockSpec((1, t, SLAB_W), lambda b, i: (b, i, 0)),
            pl.BlockSpec((1, LANES), full),
            pl.BlockSpec((1, MIX_W), full),
        ],
        out_specs=pl.BlockSpec((1, t, MIX_W), lambda b, i: (b, i, 0)),
        scratch_shapes=[
            pltpu.VMEM((HEADS, HEAD_W, 2 * HEAD_W), F32),
            pltpu.VMEM((HEADS, 8, LANES), F32),
            tile(), tile(), tile(), tile(), tile(),
        ],
        compiler_params=pltpu.CompilerParams(dimension_semantics=("parallel", "arbitrary"),
                                             vmem_limit_bytes=VMEM_LIMIT),
        name="mlstm",
    )(slab.reshape(bsz, seq, SLAB_W), bias, norm_g.reshape(1, MIX_W))
    return out.reshape(bsz * seq, MIX_W)


def _rope_tables(seq):
    inv_freq = ROPE_THETA ** (-jnp.arange(0, DIFF_DH, 2, dtype=F32) / DIFF_DH)
    ang = jnp.arange(seq, dtype=F32)[:, None] * inv_freq[None, :]
    cos, sin = jnp.cos(ang), jnp.sin(ang)
    reps = LANES // DIFF_DH
    cos_t = jnp.tile(jnp.concatenate([cos, cos], axis=-1), (1, reps))
    sin_t = jnp.tile(jnp.concatenate([-sin, sin], axis=-1), (1, reps))
    return cos_t, sin_t


def kernel(x, ffn1_pre_g, ffn1_w_gate, ffn1_w_up, ffn1_w_down, ffn1_post_g, mix_pre_g, w_in, gdn_conv_w, gdn_a_log, gdn_dt_bias, gdn_norm_g, diff_lam_q1, diff_lam_k1, diff_lam_q2, diff_lam_k2, diff_norm_g, ssd_conv_w, ssd_conv_b, ssd_a_log, ssd_dt_bias, ssd_d, ssd_norm_g, mlstm_i_bias, mlstm_f_bias, mlstm_norm_g, w_out, mix_post_g, ffn2_pre_g, ffn2_w_gate, ffn2_w_up, ffn2_w_down, ffn2_post_g):
    bsz, seq, d = x.shape
    x2 = x.reshape(bsz * seq, d)
    cos_t, sin_t = _rope_tables(seq)
    for l in range(DEPTH):
        lambda_init = 0.8 - 0.6 * math.exp(-0.3 * l)
        x2 = _ffn_call(x2, ffn1_pre_g[l], ffn1_w_gate[l], ffn1_w_up[l], ffn1_w_down[l], ffn1_post_g[l])
        gdn_slab, ssd_slab, ml_slab, dq, dk, dv = _proj_call(x2, mix_pre_g[l], w_in[l], cos_t, sin_t, seq)
        out_a = _gdn_call(gdn_slab, gdn_conv_w[l], gdn_a_log[l], gdn_dt_bias[l], gdn_norm_g[l], bsz, seq)
        lam_p = jnp.stack([diff_lam_q1[l], diff_lam_k1[l], diff_lam_q2[l], diff_lam_k2[l]])
        out_b = _attn_call(dq, dk, dv, lam_p, diff_norm_g[l], lambda_init, bsz, seq)
        out_c = _ssd_call(ssd_slab, ssd_conv_w[l], ssd_conv_b[l], ssd_a_log[l], ssd_dt_bias[l], ssd_d[l],
                          ssd_norm_g[l], bsz, seq)
        out_d = _mlstm_call(ml_slab, mlstm_i_bias[l], mlstm_f_bias[l], mlstm_norm_g[l], bsz, seq)
        x2 = _outproj_call(x2, (out_a, out_b, out_c, out_d), w_out[l], mix_post_g[l])
        x2 = _ffn_call(x2, ffn2_pre_g[l], ffn2_w_gate[l], ffn2_w_up[l], ffn2_w_down[l], ffn2_post_g[l])
    return x2.reshape(bsz, seq, d)
```

```python
import functools
import math

import jax
import jax.numpy as jnp
from jax import lax
from jax.experimental import pallas as pl
from jax.experimental.pallas import tpu as pltpu

F32 = jnp.float32
MXU_DTYPE = jnp.bfloat16

D_MODEL = 1024
D_FF = 2816
DEPTH = 2
CHUNK = 64
CHUNK_SHIFT = 6
CONV_K = 4
EPS = 1e-6
ROPE_THETA = 10000.0
HEADS = 4
HEAD_W = 64
MIX_W = HEADS * HEAD_W
DIFF_DH = 32
DIFF_DH_SHIFT = 5
LOG2_E = math.log2(math.e)
SSD_N = 128
LANES = 128
NEG = -1e30

_GDN_COLS = (0, 1032)
_DIFF_COLS = (1032, 1800)
_SSD_COLS = (1800, 2828)
_MLSTM_COLS = (2828, 3860)
SLAB_W = 1152

FFN_TM = 512
FFN_FC = 256
MIX_T = 256
ATT_TQ = 256
ATT_RB = 16
ATT_TK = 512
VMEM_LIMIT = 56 * 1024 * 1024


def _dot(a, b):
    return jnp.dot(a.astype(MXU_DTYPE), b.astype(MXU_DTYPE), preferred_element_type=F32)


def _dot_nt(a, b):
    return lax.dot_general(a.astype(MXU_DTYPE), b.astype(MXU_DTYPE), (((1,), (1,)), ((), ())),
                           preferred_element_type=F32)


def _dot_tn(a, b):
    return lax.dot_general(a.astype(MXU_DTYPE), b.astype(MXU_DTYPE), (((0,), (0,)), ((), ())),
                           preferred_element_type=F32)


def _split3(x):
    x1 = x.astype(MXU_DTYPE)
    r1 = x - x1.astype(F32)
    x2 = r1.astype(MXU_DTYPE)
    return x1, x2, (r1 - x2.astype(F32)).astype(MXU_DTYPE)


def _dot_sel(sel, x, terms=3):
    sel = sel.astype(MXU_DTYPE)
    return sum(jnp.dot(sel, xi, preferred_element_type=F32) for xi in _split3(x)[:terms])


def _dot_xsel(x, sel, terms=3):
    sel = sel.astype(MXU_DTYPE)
    return sum(jnp.dot(xi, sel, preferred_element_type=F32) for xi in _split3(x)[:terms])


def _chunk_last(x):
    t, w = x.shape
    return jnp.concatenate([jnp.broadcast_to(x[c + CHUNK - 1:c + CHUNK, :], (CHUNK, w))
                            for c in range(0, t, CHUNK)], axis=0)


def _rms(x, g):
    return x * lax.rsqrt(jnp.mean(x * x, axis=-1, keepdims=True) + EPS) * g


def _silu(x):
    return x / (1.0 + jnp.exp(-x))


def _sigmoid(x):
    return 1.0 / (1.0 + jnp.exp(-x))


def _softplus(x):
    return jnp.maximum(x, 0.0) + jnp.log(1.0 + jnp.exp(-jnp.abs(x)))


def _iota(shape, dim):
    return lax.broadcasted_iota(jnp.int32, shape, dim)


def _chunk_consts(t):
    r = _iota((t, t), 0)
    c = _iota((t, t), 1)
    same = (r >> CHUNK_SHIFT) == (c >> CHUNK_SHIFT)
    ltile = jnp.where(same & (c <= r), 1.0, 0.0).astype(F32)
    csame = jnp.where(same, 1.0, 0.0).astype(F32)
    pos = _iota((t, MIX_W), 0) & (CHUNK - 1)
    m = _iota((t, MIX_W), 1) & (CHUNK - 1)
    return ltile, csame, pos, m


def _head_expand(first_lane):
    r = _iota((LANES, MIX_W), 0)
    c = _iota((LANES, MIX_W), 1)
    return jnp.where(r == (c >> CHUNK_SHIFT) + first_lane, 1.0, 0.0).astype(F32)


def _head_ones():
    r = _iota((MIX_W, MIX_W), 0)
    c = _iota((MIX_W, MIX_W), 1)
    return jnp.where((r >> CHUNK_SHIFT) == (c >> CHUNK_SHIFT), 1.0, 0.0).astype(F32)


def _causal_conv(u, w_ref, carry_ref, buf_ref, t):
    buf_ref[0:8, :] = carry_ref[...]
    buf_ref[8:8 + t, :] = u
    carry_ref[...] = u[t - 8:t, :]
    acc = w_ref[CONV_K - 1:CONV_K, :] * u
    for j in range(CONV_K - 1):
        off = 8 - (CONV_K - 1) + j
        acc = acc + w_ref[j:j + 1, :] * buf_ref[off:off + t, :]
    return acc


def _ffn_kernel(x_ref, pre_ref, wg_ref, wu_ref, wd_ref, post_ref, o_ref, h_ref, acc_ref, *, fc):
    x = x_ref[...]
    h_ref[...] = _rms(x, pre_ref[...]).astype(h_ref.dtype)
    acc_ref[...] = jnp.zeros_like(acc_ref)

    for c0 in range(0, wg_ref.shape[1], fc):
        g = jnp.dot(h_ref[...], wg_ref[:, c0:c0 + fc], preferred_element_type=F32)
        u = jnp.dot(h_ref[...], wu_ref[:, c0:c0 + fc], preferred_element_type=F32)
        a = (_silu(g) * u).astype(wd_ref.dtype)
        acc_ref[...] += jnp.dot(a, wd_ref[c0:c0 + fc, :], preferred_element_type=F32)
    o_ref[...] = x + 0.5 * _rms(acc_ref[...], post_ref[...])


def _ffn_call(x2, pre_g, w_gate, w_up, w_down, post_g, tm=FFN_TM, fc=FFN_FC):
    m, d = x2.shape
    ff = w_gate.shape[1]
    full = lambda i: (0, 0)
    return pl.pallas_call(
        functools.partial(_ffn_kernel, fc=fc),
        out_shape=jax.ShapeDtypeStruct((m, d), F32),
        grid=(m // tm,),
        in_specs=[
            pl.BlockSpec((tm, d), lambda i: (i, 0)),
            pl.BlockSpec((1, d), full),
            pl.BlockSpec((d, ff), full),
            pl.BlockSpec((d, ff), full),
            pl.BlockSpec((ff, d), full),
            pl.BlockSpec((1, d), full),
        ],
        out_specs=pl.BlockSpec((tm, d), lambda i: (i, 0)),
        scratch_shapes=[pltpu.VMEM((tm, d), MXU_DTYPE), pltpu.VMEM((tm, d), F32)],
        compiler_params=pltpu.CompilerParams(dimension_semantics=("parallel",), vmem_limit_bytes=VMEM_LIMIT),
        name="ffn",
    )(x2, pre_g.reshape(1, d), w_gate.astype(MXU_DTYPE), w_up.astype(MXU_DTYPE), w_down.astype(MXU_DTYPE),
      post_g.reshape(1, d))


def _rope_half(x, cos, sin):
    lane = _iota(x.shape, 1)
    half = DIFF_DH // 2
    partner = jnp.where((lane & (DIFF_DH - 1)) < half, pltpu.roll(x, LANES - half, 1), pltpu.roll(x, half, 1))
    return x * cos + partner * sin


def _proj_kernel(x_ref, g_ref, wa_ref, wb_ref, wc_ref, wd_ref, cos_ref, sin_ref,
                 gdn_ref, ssd_ref, ml_ref, dq_ref, dk_ref, dv_ref):
    h = _rms(x_ref[...], g_ref[...]).astype(wa_ref.dtype)
    gdn_ref[...] = jnp.dot(h, wa_ref[...], preferred_element_type=F32)
    ssd_ref[...] = jnp.dot(h, wc_ref[...], preferred_element_type=F32)
    ml_ref[...] = jnp.dot(h, wd_ref[...], preferred_element_type=F32)
    d = jnp.dot(h, wb_ref[...], preferred_element_type=F32)
    cos = cos_ref[...]
    sin = sin_ref[...]
    scale = DIFF_DH ** -0.5 * LOG2_E
    for half in range(2):
        lo = half * LANES
        dq_ref[:, lo:lo + LANES] = (_rope_half(d[:, lo:lo + LANES], cos, sin) * scale).astype(dq_ref.dtype)
        dk_ref[:, lo:lo + LANES] = _rope_half(d[:, MIX_W + lo:MIX_W + lo + LANES], cos, sin).astype(dk_ref.dtype)
    ones = jnp.ones((d.shape[0], HEAD_W), F32)
    for h in range(HEADS):
        v_h = d[:, 2 * MIX_W + h * HEAD_W:2 * MIX_W + (h + 1) * HEAD_W]
        dv_ref[0, h * LANES:(h + 1) * LANES, :] = jnp.concatenate([v_h, ones], axis=-1).T.astype(dv_ref.dtype)


def _pad_cols(w, width):
    return jnp.pad(w, ((0, 0), (0, width - w.shape[1])))


def _proj_call(x2, pre_g, w_in, cos_t, sin_t, seq, tm=FFN_TM):
    m, d = x2.shape
    wa = _pad_cols(w_in[:, _GDN_COLS[0]:_GDN_COLS[1]], SLAB_W).astype(MXU_DTYPE)
    wb = w_in[:, _DIFF_COLS[0]:_DIFF_COLS[1]].astype(MXU_DTYPE)
    wc = _pad_cols(w_in[:, _SSD_COLS[0]:_SSD_COLS[1]], SLAB_W).astype(MXU_DTYPE)
    wd = _pad_cols(w_in[:, _MLSTM_COLS[0]:_MLSTM_COLS[1]], SLAB_W).astype(MXU_DTYPE)
    tiles_per_seq = seq // tm
    full = lambda i: (0, 0)
    row = lambda i: (i, 0)
    return pl.pallas_call(
        _proj_kernel,
        out_shape=(
            jax.ShapeDtypeStruct((m, SLAB_W), F32),
            jax.ShapeDtypeStruct((m, SLAB_W), F32),
            jax.ShapeDtypeStruct((m, SLAB_W), F32),
            jax.ShapeDtypeStruct((m, MIX_W), MXU_DTYPE),
            jax.ShapeDtypeStruct((m, MIX_W), MXU_DTYPE),
            jax.ShapeDtypeStruct((m // tm, HEADS * LANES, tm), MXU_DTYPE),
        ),
        grid=(m // tm,),
        in_specs=[
            pl.BlockSpec((tm, d), row),
            pl.BlockSpec((1, d), full),
            pl.BlockSpec((d, SLAB_W), full),
            pl.BlockSpec((d, 3 * MIX_W), full),
            pl.BlockSpec((d, SLAB_W), full),
            pl.BlockSpec((d, SLAB_W), full),
            pl.BlockSpec((tm, LANES), lambda i: (i % tiles_per_seq, 0)),
            pl.BlockSpec((tm, LANES), lambda i: (i % tiles_per_seq, 0)),
        ],
        out_specs=(
            pl.BlockSpec((tm, SLAB_W), row),
            pl.BlockSpec((tm, SLAB_W), row),
            pl.BlockSpec((tm, SLAB_W), row),
            pl.BlockSpec((tm, MIX_W), row),
            pl.BlockSpec((tm, MIX_W), row),
            pl.BlockSpec((1, HEADS * LANES, tm), lambda i: (i, 0, 0)),
        ),
        compiler_params=pltpu.CompilerParams(dimension_semantics=("parallel",), vmem_limit_bytes=VMEM_LIMIT),
        name="mixer_in_proj",
    )(x2, pre_g.reshape(1, d), wa, wb, wc, wd, cos_t, sin_t)


def _outproj_kernel(x_ref, a_ref, b_ref, c_ref, d_ref, w_ref, g_ref, o_ref):
    mixed = jnp.concatenate([a_ref[...], b_ref[...], c_ref[...], d_ref[...]], axis=-1).astype(w_ref.dtype)
    y = jnp.dot(mixed, w_ref[...], preferred_element_type=F32)
    o_ref[...] = x_ref[...] + _rms(y, g_ref[...])


def _outproj_call(x2, outs, w_out, post_g, tm=FFN_TM):
    m, d = x2.shape
    row = lambda i: (i, 0)
    full = lambda i: (0, 0)
    return pl.pallas_call(
        _outproj_kernel,
        out_shape=jax.ShapeDtypeStruct((m, d), F32),
        grid=(m // tm,),
        in_specs=[pl.BlockSpec((tm, d), row)] + [pl.BlockSpec((tm, MIX_W), row)] * 4
        + [pl.BlockSpec((d, d), full), pl.BlockSpec((1, d), full)],
        out_specs=pl.BlockSpec((tm, d), row),
        compiler_params=pltpu.CompilerParams(dimension_semantics=("parallel",), vmem_limit_bytes=VMEM_LIMIT),
        name="mixer_out_proj",
    )(x2, *outs, w_out.astype(MXU_DTYPE), post_g.reshape(1, d))


def _attn_kernel(q_ref, k_ref, vt_ref, lam_ref, g_ref, o_ref, qt_ref, m_ref, acc_ref, p_ref, s_ref, alpha_ref,
                 *, tq, tk, lambda_init):
    i = pl.program_id(1)
    q_t = q_ref[0].astype(F32).T
    dim_map = _iota((MIX_W, tq), 0) >> DIFF_DH_SHIFT
    for mp in range(2 * HEADS):
        qt_ref[mp // 2, :, (mp % 2) * tq:(mp % 2 + 1) * tq] = jnp.where(dim_map == mp, q_t, 0.0).astype(qt_ref.dtype)
    m_ref[...] = jnp.full_like(m_ref, NEG)
    acc_ref[...] = jnp.zeros_like(acc_ref)
    qpos = i * tq + (_iota((1, 2 * tq), 1) & (tq - 1))
    limit = ((qpos >> CHUNK_SHIFT) + 1) << CHUNK_SHIFT
    n_kt = (i * tq + tq + tk - 1) // tk

    def step(j, masked):
        start = pl.multiple_of(j * tk, tk)
        kt = k_ref[0, pl.ds(start, tk), :]

        def scores(h):
            s_ref[h] = jnp.dot(kt, qt_ref[h], preferred_element_type=F32)

        def piece(h, r0):
            sp = s_ref[h, r0:r0 + ATT_RB, :]
            if masked:
                key = start + r0 + _iota((ATT_RB, 1), 0)
                sp = jnp.where(key < limit, sp, NEG)
            return sp

        def softmax(h):
            mx = piece(h, 0)
            for r0 in range(ATT_RB, tk, ATT_RB):
                mx = jnp.maximum(mx, piece(h, r0))
            m_old = m_ref[h]
            m_new = jnp.maximum(m_old, jnp.max(mx, axis=0, keepdims=True))
            for r0 in range(0, tk, ATT_RB):
                p_ref[h, r0:r0 + ATT_RB, :] = jnp.exp2(piece(h, r0) - m_new).astype(p_ref.dtype)
            alpha_ref[h] = jnp.exp2(m_old - m_new)
            m_ref[h] = m_new

        def values(h):
            pv = jnp.dot(vt_ref[j, h * LANES:(h + 1) * LANES, :], p_ref[h], preferred_element_type=F32)
            acc_ref[h] = alpha_ref[h] * acc_ref[h] + pv

        scores(0)
        scores(1)
        for h in range(HEADS):
            softmax(h)
            if h + 2 < HEADS:
                scores(h + 2)
            values(h)

    def body(j, carry):
        step(j, False)
        return carry

    lax.fori_loop(0, n_kt - 1, body, 0)
    step(n_kt - 1, True)

    lam_p = lam_ref[...]
    lam = (jnp.exp(jnp.sum(lam_p[0:1] * lam_p[1:2], axis=-1, keepdims=True))
           - jnp.exp(jnp.sum(lam_p[2:3] * lam_p[3:4], axis=-1, keepdims=True)) + lambda_init)
    outs = []
    for h in range(HEADS):
        a = acc_ref[h]
        o = (a[:HEAD_W, :tq] / a[HEAD_W:HEAD_W + 1, :tq]
             - lam * (a[:HEAD_W, tq:] / a[HEAD_W:HEAD_W + 1, tq:]))
        outs.append(o * lax.rsqrt(jnp.mean(o * o, axis=0, keepdims=True) + EPS))
    o_ref[0] = jnp.concatenate(outs, axis=0).T * g_ref[...] * (1.0 - lambda_init)


def _attn_call(dq, dk, dvt, lam_p, norm_g, lambda_init, bsz, seq, tq=ATT_TQ, tk=ATT_TK):
    q3 = dq.reshape(bsz, seq, MIX_W)
    k3 = dk.reshape(bsz, seq, MIX_W)
    n_kt = seq // tk
    out = pl.pallas_call(
        functools.partial(_attn_kernel, tq=tq, tk=tk, lambda_init=lambda_init),
        out_shape=jax.ShapeDtypeStruct((bsz, seq, MIX_W), F32),
        grid=(bsz, seq // tq),
        in_specs=[
            pl.BlockSpec((1, tq, MIX_W), lambda b, i: (b, i, 0)),
            pl.BlockSpec((1, seq, MIX_W), lambda b, i: (b, 0, 0)),
            pl.BlockSpec((n_kt, HEADS * LANES, tk), lambda b, i: (b, 0, 0)),
            pl.BlockSpec((4, DIFF_DH), lambda b, i: (0, 0)),
            pl.BlockSpec((1, MIX_W), lambda b, i: (0, 0)),
        ],
        out_specs=pl.BlockSpec((1, tq, MIX_W), lambda b, i: (b, i, 0)),
        scratch_shapes=[
            pltpu.VMEM((HEADS, MIX_W, 2 * tq), MXU_DTYPE),
            pltpu.VMEM((HEADS, 1, 2 * tq), F32),
            pltpu.VMEM((HEADS, LANES, 2 * tq), F32),
            pltpu.VMEM((HEADS, tk, 2 * tq), MXU_DTYPE),
            pltpu.VMEM((HEADS, tk, 2 * tq), F32),
            pltpu.VMEM((HEADS, 1, 2 * tq), F32),
        ],
        compiler_params=pltpu.CompilerParams(dimension_semantics=("parallel", "arbitrary"),
                                             vmem_limit_bytes=VMEM_LIMIT),
        name="diff_attention",
    )(q3, k3, dvt, lam_p, jnp.tile(norm_g, HEADS).reshape(1, MIX_W))
    return out.reshape(bsz * seq, MIX_W)


def _ssd_kernel(slab_ref, cw_ref, cb_ref, alog_ref, dtb_ref, dskip_ref, g_ref, y_ref,
                carry_ref, buf_ref, st_ref, xc_ref, xcd_ref, bm_ref, cm_ref, lm_ref, ea_ref, el_ref, yo_ref, *, t):
    @pl.when(pl.program_id(1) == 0)
    def _():
        carry_ref[...] = jnp.zeros_like(carry_ref)
        st_ref[...] = jnp.zeros_like(st_ref)

    ltile, _, pos, mm = _chunk_consts(t)
    expand = _head_expand(0)
    z = slab_ref[0, :, 0:MIX_W]
    xbc = _silu(_causal_conv(slab_ref[0, :, MIX_W:4 * MIX_W], cw_ref, carry_ref, buf_ref, t) + cb_ref[...])
    x = xbc[:, 0:MIX_W]
    bm_ref[...] = xbc[:, MIX_W:2 * MIX_W].astype(bm_ref.dtype)
    cm_ref[...] = xbc[:, 2 * MIX_W:3 * MIX_W].astype(cm_ref.dtype)
    dt = _softplus(slab_ref[0, :, 4 * MIX_W:] + dtb_ref[...])
    da = dt * (-jnp.exp(alog_ref[...]))
    dt_e = _dot_xsel(dt, expand)
    da_e = _dot_xsel(da, expand)
    acs_e = _dot_sel(ltile,da_e)
    acs_last_e = _chunk_last(acs_e)
    ldiff = _dot_sel(ltile,jnp.where(pos > mm, da_e, 0.0))
    lm_ref[...] = jnp.where(mm <= pos, jnp.exp(ldiff), 0.0)
    xc = x * dt_e
    xc_ref[...] = xc.astype(xc_ref.dtype)
    xcd_ref[...] = (xc * jnp.exp(acs_last_e - acs_e)).astype(xcd_ref.dtype)
    ea_ref[...] = jnp.exp(acs_e)
    el_ref[...] = jnp.exp(acs_last_e)

    for c in range(t // CHUNK):
        r0 = c * CHUNK
        rows = slice(r0, r0 + CHUNK)
        for grp in range(2):
            gl = slice(grp * SSD_N, (grp + 1) * SSD_N)
            bg = bm_ref[rows, gl]
            cg = cm_ref[rows, gl]
            cb = _dot_nt(cg, bg)
            ys = []
            for hh in range(2):
                hl = slice(grp * SSD_N + hh * HEAD_W, grp * SSD_N + (hh + 1) * HEAD_W)
                ys.append(_dot(cb * lm_ref[rows, hl], xc_ref[rows, hl]))
            y_diag = jnp.concatenate(ys, axis=-1)
            st = st_ref[grp]
            y_off = _dot(cg, st) * ea_ref[rows, gl]
            st_ref[grp] = st * el_ref[r0:r0 + 1, gl] + _dot_tn(bg, xcd_ref[rows, gl])
            yo_ref[rows, gl] = y_diag + y_off

    y = (yo_ref[...] + x * dskip_ref[...]) * _silu(z)
    gs = g_ref[...]
    for grp in range(2):
        gl = slice(grp * SSD_N, (grp + 1) * SSD_N)
        y_ref[0, :, gl] = _rms(y[:, gl], gs[:, gl])


def _ssd_call(slab, conv_w, conv_b, a_log, dt_bias, d_skip, norm_g, bsz, seq, t=MIX_T):
    ch = 3 * MIX_W
    lane_pad = lambda v: jnp.pad(v, (0, LANES - v.shape[0])).reshape(1, LANES)
    full = lambda b, i: (0, 0)
    out = pl.pallas_call(
        functools.partial(_ssd_kernel, t=t),
        out_shape=jax.ShapeDtypeStruct((bsz, seq, MIX_W), F32),
        grid=(bsz, seq // t),
        in_specs=[
            pl.BlockSpec((1, t, SLAB_W), lambda b, i: (b, i, 0)),
            pl.BlockSpec((CONV_K, ch), full),
            pl.BlockSpec((1, ch), full),
            pl.BlockSpec((1, LANES), full),
            pl.BlockSpec((1, LANES), full),
            pl.BlockSpec((1, MIX_W), full),
            pl.BlockSpec((1, MIX_W), full),
        ],
        out_specs=pl.BlockSpec((1, t, MIX_W), lambda b, i: (b, i, 0)),
        scratch_shapes=[
            pltpu.VMEM((8, ch), F32),
            pltpu.VMEM((t + 8, ch), F32),
            pltpu.VMEM((2, SSD_N, SSD_N), F32),
            pltpu.VMEM((t, MIX_W), MXU_DTYPE),
            pltpu.VMEM((t, MIX_W), MXU_DTYPE),
            pltpu.VMEM((t, MIX_W), MXU_DTYPE),
            pltpu.VMEM((t, MIX_W), MXU_DTYPE),
            pltpu.VMEM((t, MIX_W), F32),
            pltpu.VMEM((t, MIX_W), F32),
            pltpu.VMEM((t, MIX_W), F32),
            pltpu.VMEM((t, MIX_W), F32),
        ],
        compiler_params=pltpu.CompilerParams(dimension_semantics=("parallel", "arbitrary"),
                                             vmem_limit_bytes=VMEM_LIMIT),
        name="ssd_mixer",
    )(slab.reshape(bsz, seq, SLAB_W), conv_w, conv_b.reshape(1, ch), lane_pad(a_log), lane_pad(dt_bias),
      jnp.repeat(d_skip, HEAD_W).reshape(1, MIX_W), norm_g.reshape(1, MIX_W))
    return out.reshape(bsz * seq, MIX_W)


def _bd(x, mask):
    x4 = jnp.concatenate([x] * HEADS, axis=0)
    return jnp.where(mask, x4, jnp.zeros_like(x4))


def _split(x):
    hi = x.astype(MXU_DTYPE)
    return hi, (x - hi.astype(F32)).astype(MXU_DTYPE)


def _bd_matmul3(a, b, mask):
    ah, al = _split(a)
    bh, bl = _split(b)
    m = a.shape[0]
    r = jnp.dot(jnp.concatenate([ah, al], axis=0), _bd(bh, mask), preferred_element_type=F32)
    return r[:m] + r[m:] + jnp.dot(ah, _bd(bl, mask), preferred_element_type=F32)


def _gdn_kernel(slab_ref, cw_ref, alog_ref, dtb_ref, g_ref, y_ref,
                carry_ref, buf_ref, st_ref, qn_ref, kn_ref, qd_ref, kd_ref, kb_ref, vb_ref, kbg_ref, dec_ref,
                egl_ref, u_ref, w_ref, attn_ref, o_ref, *, t):
    @pl.when(pl.program_id(1) == 0)
    def _():
        carry_ref[...] = jnp.zeros_like(carry_ref)
        st_ref[...] = jnp.zeros_like(st_ref)

    ltile, _, pos, mm = _chunk_consts(t)
    ones_h = _head_ones()
    qkv = _silu(_causal_conv(slab_ref[0, :, 0:3 * MIX_W], cw_ref, carry_ref, buf_ref, t))
    q = qkv[:, 0:MIX_W]
    k = qkv[:, MIX_W:2 * MIX_W]
    v = qkv[:, 2 * MIX_W:]
    qn = q * lax.rsqrt(_dot_xsel(q * q, ones_h, terms=2) + EPS) * (HEAD_W ** -0.5)
    kn = k * lax.rsqrt(_dot_xsel(k * k, ones_h, terms=2) + EPS)
    ba = slab_ref[0, :, 4 * MIX_W:]
    beta_e = _dot_xsel(_sigmoid(ba), _head_expand(0))
    g4 = -jnp.exp(alog_ref[...]) * _softplus(ba + dtb_ref[...])
    g_e = _dot_xsel(g4, _head_expand(HEADS))
    gc_e = _dot_sel(ltile,g_e)
    gl_e = _chunk_last(gc_e)
    diff = _dot_sel(ltile,jnp.where(pos > mm, g_e, 0.0))
    dec_ref[...] = jnp.where(mm <= pos, jnp.exp(diff), 0.0)
    egc = jnp.exp(gc_e)
    kb = kn * beta_e
    qn_ref[...] = qn
    kn_ref[...] = kn
    qd_ref[...] = qn * egc
    kd_ref[...] = kn * jnp.exp(gl_e - gc_e)
    kb_ref[...] = kb
    vb_ref[...] = v * beta_e
    kbg_ref[...] = kb * egc
    egl_ref[...] = jnp.exp(gl_e)

    head_mask = (_iota((MIX_W, MIX_W), 0) >> CHUNK_SHIFT) == (_iota((MIX_W, MIX_W), 1) >> CHUNK_SHIFT)
    r_ss = _iota((CHUNK, MIX_W), 0)
    m_ss = _iota((CHUNK, MIX_W), 1) & (CHUNK - 1)
    eye_ss = jnp.where(r_ss == m_ss, 1.0, 0.0).astype(F32)
    n_chunks = t // CHUNK
    chunk_rows = [slice(c * CHUNK, (c + 1) * CHUNK) for c in range(n_chunks)]

    ys, tms = [], []
    for rows in chunk_rows:
        lhs = jnp.concatenate([kb_ref[rows, :], qn_ref[rows, :]], axis=0).astype(MXU_DTYPE)
        raw = lax.dot_general(lhs, _bd(kn_ref[rows, :].astype(MXU_DTYPE), head_mask), (((1,), (1,)), ((), ())),
                              preferred_element_type=F32)
        dec = dec_ref[rows, :]
        attn_ref[rows, :] = raw[CHUNK:] * dec
        y = jnp.where(m_ss < r_ss, -(raw[:CHUNK] * dec), 0.0)
        ys.append(y)
        tms.append(eye_ss + y)
    ys = [_bd_matmul3(y, y, head_mask) for y in ys]
    for _ in range(4):
        rs = [_bd_matmul3(jnp.concatenate([tm, y], axis=0), y, head_mask) for tm, y in zip(tms, ys)]
        tms = [tm + r[:CHUNK] for tm, r in zip(tms, rs)]
        ys = [r[CHUNK:] for r in rs]
    tms = [tm + _bd_matmul3(tm, y, head_mask) for tm, y in zip(tms, ys)]
    for rows, tm in zip(chunk_rows, tms):
        rhs = jnp.concatenate([_bd(vb_ref[rows, :].astype(MXU_DTYPE), head_mask),
                               _bd(kbg_ref[rows, :].astype(MXU_DTYPE), head_mask)], axis=1)
        uw = jnp.dot(tm.astype(MXU_DTYPE), rhs, preferred_element_type=F32)
        u_ref[rows, :] = uw[:, :MIX_W]
        w_ref[rows, :] = uw[:, MIX_W:]

    st = st_ref[...]
    for c, rows in enumerate(chunk_rows):
        lhs = jnp.concatenate([w_ref[rows, :], qd_ref[rows, :]], axis=0).astype(MXU_DTYPE)
        r = jnp.dot(lhs, st.astype(MXU_DTYPE), preferred_element_type=F32)
        v_new = u_ref[rows, :] - r[:CHUNK]
        v_new_c = v_new.astype(MXU_DTYPE)
        o_ref[rows, :] = r[CHUNK:] + jnp.dot(attn_ref[rows, :].astype(MXU_DTYPE), _bd(v_new_c, head_mask),
                                             preferred_element_type=F32)
        kv = lax.dot_general(kd_ref[rows, :].astype(MXU_DTYPE), v_new_c, (((0,), (0,)), ((), ())),
                             preferred_element_type=F32)
        st = st * egl_ref[c * CHUNK:c * CHUNK + 1, :] + jnp.where(head_mask, kv, 0.0)
    st_ref[...] = st

    o = o_ref[...]
    o = o * lax.rsqrt(_dot_xsel(o * o, ones_h, terms=2) * (1.0 / HEAD_W) + EPS) * g_ref[...]
    y_ref[0] = o * _silu(slab_ref[0, :, 3 * MIX_W:4 * MIX_W])


def _gdn_call(slab, conv_w, a_log, dt_bias, norm_g, bsz, seq, t=MIX_T):
    ch = 3 * MIX_W
    gate_pad = lambda v: jnp.pad(v, (HEADS, LANES - 2 * HEADS)).reshape(1, LANES)
    full = lambda b, i: (0, 0)
    tile = lambda: pltpu.VMEM((t, MIX_W), F32)
    out = pl.pallas_call(
        functools.partial(_gdn_kernel, t=t),
        out_shape=jax.ShapeDtypeStruct((bsz, seq, MIX_W), F32),
        grid=(bsz, seq // t),
        in_specs=[
            pl.BlockSpec((1, t, SLAB_W), lambda b, i: (b, i, 0)),
            pl.BlockSpec((CONV_K, ch), full),
            pl.BlockSpec((1, LANES), full),
            pl.BlockSpec((1, LANES), full),
            pl.BlockSpec((1, MIX_W), full),
        ],
        out_specs=pl.BlockSpec((1, t, MIX_W), lambda b, i: (b, i, 0)),
        scratch_shapes=[
            pltpu.VMEM((8, ch), F32),
            pltpu.VMEM((t + 8, ch), F32),
            pltpu.VMEM((MIX_W, MIX_W), F32),
            tile(), tile(), tile(), tile(), tile(), tile(), tile(), tile(), tile(), tile(), tile(), tile(), tile(),
        ],
        compiler_params=pltpu.CompilerParams(dimension_semantics=("parallel", "arbitrary"),
                                             vmem_limit_bytes=VMEM_LIMIT),
        name="gated_deltanet",
    )(slab.reshape(bsz, seq, SLAB_W), conv_w, gate_pad(a_log), gate_pad(dt_bias),
      jnp.tile(norm_g, HEADS).reshape(1, MIX_W))
    return out.reshape(bsz * seq, MIX_W)


def _mlstm_kernel(slab_ref, bias_ref, g_ref, y_ref, st_ref, m_ref, dl_ref, b_ref, ge_ref, dmax_ref, qk_ref,
                  o_ref, *, t):
    @pl.when(pl.program_id(1) == 0)
    def _():
        st_ref[...] = jnp.zeros_like(st_ref)
        m_ref[...] = jnp.zeros_like(m_ref)

    ltile, csame, pos, mm = _chunk_consts(t)
    gates = slab_ref[0, :, 4 * MIX_W:] + bias_ref[...]
    li_e = _dot_xsel(gates, _head_expand(0))
    lf_e = _dot_xsel(-_softplus(-gates), _head_expand(HEADS))
    b_e = _dot_sel(ltile, lf_e)
    dlog = _dot_sel(ltile, jnp.where(pos > mm, lf_e, 0.0)) + _dot_sel(csame, jnp.where(pos == mm, li_e, 0.0))
    dl = jnp.where(mm <= pos, dlog, NEG)
    lane_head = _iota((t, MIX_W), 1) >> CHUNK_SHIFT
    dmax = jnp.full((t, MIX_W), NEG, F32)
    for h in range(HEADS):
        in_head = lane_head == h
        dmax = jnp.where(in_head, jnp.max(jnp.where(in_head, dl, NEG), axis=-1, keepdims=True), dmax)
    dl_ref[...] = dl
    dmax_ref[...] = dmax
    b_ref[...] = b_e
    ge_ref[...] = _chunk_last(b_e) - b_e + li_e

    head_mask = (_iota((MIX_W, MIX_W), 0) >> CHUNK_SHIFT) == (_iota((MIX_W, MIX_W), 1) >> CHUNK_SHIFT)
    head_mask2 = jnp.concatenate([head_mask, head_mask], axis=1)
    ones_bd = jnp.where(head_mask, 1.0, 0.0).astype(MXU_DTYPE)
    ones_rows = jnp.ones((CHUNK, MIX_W), MXU_DTYPE)
    chunk_rows = [slice(c * CHUNK, (c + 1) * CHUNK) for c in range(t // CHUNK)]
    k_scale = HEAD_W ** -0.5
    for rows in chunk_rows:
        k_c = (slab_ref[0, rows, MIX_W:2 * MIX_W] * k_scale).astype(MXU_DTYPE)
        qk_ref[rows, :] = lax.dot_general(slab_ref[0, rows, 0:MIX_W].astype(MXU_DTYPE), _bd(k_c, head_mask),
                                          (((1,), (1,)), ((), ())), preferred_element_type=F32)

    st = st_ref[...]
    m_st = m_ref[...]
    for rows in chunk_rows:
        q_c = slab_ref[0, rows, 0:MIX_W].astype(MXU_DTYPE)
        k_c = slab_ref[0, rows, MIX_W:2 * MIX_W] * k_scale
        v_c = slab_ref[0, rows, 2 * MIX_W:3 * MIX_W].astype(MXU_DTYPE)
        b_c = b_ref[rows, :]
        m_t = jnp.maximum(b_c + m_st, dmax_ref[rows, :])
        w_st = jnp.exp(b_c + m_st - m_t)
        s = qk_ref[rows, :] * jnp.exp(dl_ref[rows, :] - m_t)
        qs = jnp.dot(q_c, st.astype(MXU_DTYPE), preferred_element_type=F32)
        sv = jnp.dot(s.astype(MXU_DTYPE), jnp.concatenate([_bd(v_c, head_mask), ones_bd], axis=1),
                     preferred_element_type=F32)
        num = w_st * qs[:, :MIX_W] + sv[:, :MIX_W]
        den = w_st * qs[:, MIX_W:] + sv[:, MIX_W:]
        o_ref[rows, :] = num / jnp.maximum(jnp.abs(den), jnp.exp(-m_t))
        ge_c = ge_ref[rows, :]
        b_last = b_c[CHUNK - 1:CHUNK, :]
        m_new = jnp.maximum(b_last + m_st, jnp.max(ge_c, axis=0, keepdims=True))
        w_old = jnp.exp(b_last + m_st - m_new)
        k_w = (k_c * jnp.exp(ge_c - m_new)).astype(MXU_DTYPE)
        upd = lax.dot_general(k_w, jnp.concatenate([v_c, ones_rows], axis=1), (((0,), (0,)), ((), ())),
                              preferred_element_type=F32)
        st = st * jnp.concatenate([w_old, w_old], axis=1) + jnp.where(head_mask2, upd, 0.0)
        m_st = m_new
    st_ref[...] = st
    m_ref[...] = m_st

    hcat = o_ref[...] * _sigmoid(slab_ref[0, :, 3 * MIX_W:4 * MIX_W])
    ms = _dot_xsel(hcat * hcat, _head_ones(), terms=2) * (1.0 / HEAD_W)
    y_ref[0] = hcat * lax.rsqrt(ms + EPS) * g_ref[...]


def _mlstm_call(slab, i_bias, f_bias, norm_g, bsz, seq, t=MIX_T):
    bias = jnp.pad(jnp.concatenate([i_bias, f_bias]), (0, LANES - 2 * HEADS)).reshape(1, LANES)
    full = lambda b, i: (0, 0)
    tile = lambda: pltpu.VMEM((t, MIX_W), F32)
    out = pl.pallas_call(
        functools.partial(_mlstm_kernel, t=t),
        out_shape=jax.ShapeDtypeStruct((bsz, seq, MIX_W), F32),
        grid=(bsz, seq // t),
        in_specs=[
            pl.BlockSpec((1, t, SLAB_W), lambda b, i: (b, i, 0)),
            pl.BlockSpec((1, LANES), full),
            pl.BlockSpec((1, MIX_W), full),
        ],
        out_specs=pl.BlockSpec((1, t, MIX_W), lambda b, i: (b, i, 0)),
        scratch_shapes=[
            pltpu.VMEM((MIX_W, 2 * MIX_W), F32),
            pltpu.VMEM((1, MIX_W), F32),
            tile(), tile(), tile(), tile(), tile(), tile(),
        ],
        compiler_params=pltpu.CompilerParams(dimension_semantics=("parallel", "arbitrary"),
                                             vmem_limit_bytes=VMEM_LIMIT),
        name="mlstm",
    )(slab.reshape(bsz, seq, SLAB_W), bias, norm_g.reshape(1, MIX_W))
    return out.reshape(bsz * seq, MIX_W)


def _rope_tables(seq):
    inv_freq = ROPE_THETA ** (-jnp.arange(0, DIFF_DH, 2, dtype=F32) / DIFF_DH)
    ang = jnp.arange(seq, dtype=F32)[:, None] * inv_freq[None, :]
    cos, sin = jnp.cos(ang), jnp.sin(ang)
    reps = LANES // DIFF_DH
    cos_t = jnp.tile(jnp.concatenate([cos, cos], axis=-1), (1, reps))
    sin_t = jnp.tile(jnp.concatenate([-sin, sin], axis=-1), (1, reps))
    return cos_t, sin_t


def kernel(x, ffn1_pre_g, ffn1_w_gate, ffn1_w_up, ffn1_w_down, ffn1_post_g, mix_pre_g, w_in, gdn_conv_w, gdn_a_log, gdn_dt_bias, gdn_norm_g, diff_lam_q1, diff_lam_k1, diff_lam_q2, diff_lam_k2, diff_norm_g, ssd_conv_w, ssd_conv_b, ssd_a_log, ssd_dt_bias, ssd_d, ssd_norm_g, mlstm_i_bias, mlstm_f_bias, mlstm_norm_g, w_out, mix_post_g, ffn2_pre_g, ffn2_w_gate, ffn2_w_up, ffn2_w_down, ffn2_post_g):
    bsz, seq, d = x.shape
    x2 = x.reshape(bsz * seq, d)
    cos_t, sin_t = _rope_tables(seq)
    for l in range(DEPTH):
        lambda_init = 0.8 - 0.6 * math.exp(-0.3 * l)
        x2 = _ffn_call(x2, ffn1_pre_g[l], ffn1_w_gate[l], ffn1_w_up[l], ffn1_w_down[l], ffn1_post_g[l])
        gdn_slab, ssd_slab, ml_slab, dq, dk, dv = _proj_call(x2, mix_pre_g[l], w_in[l], cos_t, sin_t, seq)
        out_a = _gdn_call(gdn_slab, gdn_conv_w[l], gdn_a_log[l], gdn_dt_bias[l], gdn_norm_g[l], bsz, seq)
        lam_p = jnp.stack([diff_lam_q1[l], diff_lam_k1[l], diff_lam_q2[l], diff_lam_k2[l]])
        out_b = _attn_call(dq, dk, dv, lam_p, diff_norm_g[l], lambda_init, bsz, seq)
        out_c = _ssd_call(ssd_slab, ssd_conv_w[l], ssd_conv_b[l], ssd_a_log[l], ssd_dt_bias[l], ssd_d[l],
                          ssd_norm_g[l], bsz, seq)
        out_d = _mlstm_call(ml_slab, mlstm_i_bias[l], mlstm_f_bias[l], mlstm_norm_g[l], bsz, seq)
        x2 = _outproj_call(x2, (out_a, out_b, out_c, out_d), w_out[l], mix_post_g[l])
        x2 = _ffn_call(x2, ffn2_pre_g[l], ffn2_w_gate[l], ffn2_w_up[l], ffn2_w_down[l], ffn2_post_g[l])
    return x2.reshape(bsz, seq, d)
```

```python
import functools
import math

import jax
import jax.numpy as jnp
from jax import lax
from jax.experimental import pallas as pl
from jax.experimental.pallas import tpu as pltpu

F32 = jnp.float32
MXU_DTYPE = jnp.bfloat16

D_MODEL = 1024
D_FF = 2816
DEPTH = 2
CHUNK = 64
CHUNK_SHIFT = 6
CONV_K = 4
EPS = 1e-6
ROPE_THETA = 10000.0
HEADS = 4
HEAD_W = 64
MIX_W = HEADS * HEAD_W
DIFF_DH = 32
DIFF_DH_SHIFT = 5
LOG2_E = math.log2(math.e)
SSD_N = 128
LANES = 128
NEG = -1e30

_GDN_COLS = (0, 1032)
_DIFF_COLS = (1032, 1800)
_SSD_COLS = (1800, 2828)
_MLSTM_COLS = (2828, 3860)
SLAB_W = 1152

FFN_TM = 512
FFN_FC = 256
MIX_T = 256
ATT_TQ = 256
ATT_RB = 16
ATT_TK = FFN_TM
VMEM_LIMIT = 56 * 1024 * 1024


def _dot(a, b):
    return jnp.dot(a.astype(MXU_DTYPE), b.astype(MXU_DTYPE), preferred_element_type=F32)


def _dot_nt(a, b):
    return lax.dot_general(a.astype(MXU_DTYPE), b.astype(MXU_DTYPE), (((1,), (1,)), ((), ())),
                           preferred_element_type=F32)


def _dot_tn(a, b):
    return lax.dot_general(a.astype(MXU_DTYPE), b.astype(MXU_DTYPE), (((0,), (0,)), ((), ())),
                           preferred_element_type=F32)


def _split3(x):
    x1 = x.astype(MXU_DTYPE)
    r1 = x - x1.astype(F32)
    x2 = r1.astype(MXU_DTYPE)
    return x1, x2, (r1 - x2.astype(F32)).astype(MXU_DTYPE)


def _dot_sel(sel, x, terms=3):
    sel = sel.astype(MXU_DTYPE)
    return sum(jnp.dot(sel, xi, preferred_element_type=F32) for xi in _split3(x)[:terms])


def _dot_xsel(x, sel, terms=3):
    sel = sel.astype(MXU_DTYPE)
    return sum(jnp.dot(xi, sel, preferred_element_type=F32) for xi in _split3(x)[:terms])


def _chunk_last(x):
    t, w = x.shape
    return jnp.concatenate([jnp.broadcast_to(x[c + CHUNK - 1:c + CHUNK, :], (CHUNK, w))
                            for c in range(0, t, CHUNK)], axis=0)


def _rms(x, g):
    return x * lax.rsqrt(jnp.mean(x * x, axis=-1, keepdims=True) + EPS) * g


def _silu(x):
    return x / (1.0 + jnp.exp(-x))


def _sigmoid(x):
    return 1.0 / (1.0 + jnp.exp(-x))


def _softplus(x):
    return jnp.maximum(x, 0.0) + jnp.log(1.0 + jnp.exp(-jnp.abs(x)))


def _iota(shape, dim):
    return lax.broadcasted_iota(jnp.int32, shape, dim)


def _chunk_consts(t):
    r = _iota((t, t), 0)
    c = _iota((t, t), 1)
    same = (r >> CHUNK_SHIFT) == (c >> CHUNK_SHIFT)
    ltile = jnp.where(same & (c <= r), 1.0, 0.0).astype(F32)
    csame = jnp.where(same, 1.0, 0.0).astype(F32)
    pos = _iota((t, MIX_W), 0) & (CHUNK - 1)
    m = _iota((t, MIX_W), 1) & (CHUNK - 1)
    return ltile, csame, pos, m


def _head_expand(first_lane):
    r = _iota((LANES, MIX_W), 0)
    c = _iota((LANES, MIX_W), 1)
    return jnp.where(r == (c >> CHUNK_SHIFT) + first_lane, 1.0, 0.0).astype(F32)


def _head_mask():
    return (_iota((MIX_W, MIX_W), 0) >> CHUNK_SHIFT) == (_iota((MIX_W, MIX_W), 1) >> CHUNK_SHIFT)


def _head_ones():
    return jnp.where(_head_mask(), 1.0, 0.0).astype(F32)


def _causal_conv(u, w_ref, carry_ref, buf_ref, t):
    buf_ref[0:8, :] = carry_ref[...]
    buf_ref[8:8 + t, :] = u
    carry_ref[...] = u[t - 8:t, :]
    acc = w_ref[CONV_K - 1:CONV_K, :] * u
    for j in range(CONV_K - 1):
        off = 8 - (CONV_K - 1) + j
        acc = acc + w_ref[j:j + 1, :] * buf_ref[off:off + t, :]
    return acc


def _ffn_body(x, pre_ref, wg_ref, wu_ref, wd_ref, post_ref, h_ref, acc_ref, fc):
    h_ref[...] = _rms(x, pre_ref[...]).astype(h_ref.dtype)
    acc_ref[...] = jnp.zeros_like(acc_ref)
    for c0 in range(0, wg_ref.shape[1], fc):
        g = jnp.dot(h_ref[...], wg_ref[:, c0:c0 + fc], preferred_element_type=F32)
        u = jnp.dot(h_ref[...], wu_ref[:, c0:c0 + fc], preferred_element_type=F32)
        a = (_silu(g) * u).astype(wd_ref.dtype)
        acc_ref[...] += jnp.dot(a, wd_ref[c0:c0 + fc, :], preferred_element_type=F32)
    return x + 0.5 * _rms(acc_ref[...], post_ref[...])


def _ffn_kernel(x_ref, pre_ref, wg_ref, wu_ref, wd_ref, post_ref, o_ref, h_ref, acc_ref, *, fc):
    o_ref[...] = _ffn_body(x_ref[...], pre_ref, wg_ref, wu_ref, wd_ref, post_ref, h_ref, acc_ref, fc)


def _outproj_ffn_kernel(x_ref, a_ref, b_ref, c_ref, d_ref, wo_ref, go_ref, pre_ref, wg_ref, wu_ref, wd_ref, post_ref,
                        o_ref, h_ref, acc_ref, *, fc):
    mixed = jnp.concatenate([a_ref[...], b_ref[...], c_ref[...], d_ref[...]], axis=-1).astype(wo_ref.dtype)
    x = x_ref[...] + _rms(jnp.dot(mixed, wo_ref[...], preferred_element_type=F32), go_ref[...])
    o_ref[...] = _ffn_body(x, pre_ref, wg_ref, wu_ref, wd_ref, post_ref, h_ref, acc_ref, fc)


def _resident(shape):
    return pl.BlockSpec(shape, lambda i: (0,) * len(shape), pipeline_mode=pl.Buffered(1))


def _ffn_call(x2, pre_g, w_gate, w_up, w_down, post_g, mix=None, tm=FFN_TM, fc=FFN_FC):
    m, d = x2.shape
    ff = w_gate.shape[1]
    row = lambda i: (i, 0)
    ffn_specs = [_resident((1, d)), _resident((d, ff)), _resident((d, ff)), _resident((ff, d)), _resident((1, d))]
    ffn_args = (pre_g.reshape(1, d), w_gate.astype(MXU_DTYPE), w_up.astype(MXU_DTYPE), w_down.astype(MXU_DTYPE),
                post_g.reshape(1, d))
    if mix is None:
        body, name = _ffn_kernel, "ffn"
        specs, args = ffn_specs, ffn_args
    else:
        outs, w_out, mix_post_g = mix
        body, name = _outproj_ffn_kernel, "mixer_out_proj_ffn"
        specs = [pl.BlockSpec((tm, MIX_W), row)] * 4 + [_resident((d, d)), _resident((1, d))] + ffn_specs
        args = (*outs, w_out.astype(MXU_DTYPE), mix_post_g.reshape(1, d)) + ffn_args
    return pl.pallas_call(
        functools.partial(body, fc=fc),
        out_shape=jax.ShapeDtypeStruct((m, d), F32),
        grid=(m // tm,),
        in_specs=[pl.BlockSpec((tm, d), row)] + specs,
        out_specs=pl.BlockSpec((tm, d), row),
        scratch_shapes=[pltpu.VMEM((tm, d), MXU_DTYPE), pltpu.VMEM((tm, d), F32)],
        compiler_params=pltpu.CompilerParams(dimension_semantics=("parallel",), vmem_limit_bytes=VMEM_LIMIT),
        name=name,
    )(x2, *args)


def _rope_half(x, cos, sin):
    lane = _iota(x.shape, 1)
    half = DIFF_DH // 2
    partner = jnp.where((lane & (DIFF_DH - 1)) < half, pltpu.roll(x, LANES - half, 1), pltpu.roll(x, half, 1))
    return x * cos + partner * sin


def _proj_kernel(x_ref, g_ref, wa_ref, wb_ref, wc_ref, wd_ref, cos_ref, sin_ref,
                 gdn_ref, ssd_ref, ml_ref, dq_ref, dk_ref, dv_ref):
    h = _rms(x_ref[...], g_ref[...]).astype(wa_ref.dtype)
    gdn_ref[...] = jnp.dot(h, wa_ref[...], preferred_element_type=F32)
    ssd_ref[...] = jnp.dot(h, wc_ref[...], preferred_element_type=F32)
    ml_ref[...] = jnp.dot(h, wd_ref[...], preferred_element_type=F32)
    d = jnp.dot(h, wb_ref[...], preferred_element_type=F32)
    cos = cos_ref[...]
    sin = sin_ref[...]
    scale = DIFF_DH ** -0.5 * LOG2_E
    for half in range(2):
        lo = half * LANES
        dq_ref[:, lo:lo + LANES] = (_rope_half(d[:, lo:lo + LANES], cos, sin) * scale).astype(dq_ref.dtype)
        dk_ref[:, lo:lo + LANES] = _rope_half(d[:, MIX_W + lo:MIX_W + lo + LANES], cos, sin).astype(dk_ref.dtype)
    ones = jnp.ones((d.shape[0], HEAD_W), F32)
    for h in range(HEADS):
        v_h = d[:, 2 * MIX_W + h * HEAD_W:2 * MIX_W + (h + 1) * HEAD_W]
        dv_ref[0, h * LANES:(h + 1) * LANES, :] = jnp.concatenate([v_h, ones], axis=-1).T.astype(dv_ref.dtype)


def _pad_cols(w, width):
    return jnp.pad(w, ((0, 0), (0, width - w.shape[1])))


def _proj_call(x2, pre_g, w_in, cos_t, sin_t, seq, tm=FFN_TM):
    m, d = x2.shape
    wa = _pad_cols(w_in[:, _GDN_COLS[0]:_GDN_COLS[1]], SLAB_W).astype(MXU_DTYPE)
    wb = w_in[:, _DIFF_COLS[0]:_DIFF_COLS[1]].astype(MXU_DTYPE)
    wc = _pad_cols(w_in[:, _SSD_COLS[0]:_SSD_COLS[1]], SLAB_W).astype(MXU_DTYPE)
    wd = _pad_cols(w_in[:, _MLSTM_COLS[0]:_MLSTM_COLS[1]], SLAB_W).astype(MXU_DTYPE)
    tiles_per_seq = seq // tm
    full = lambda i: (0, 0)
    row = lambda i: (i, 0)
    return pl.pallas_call(
        _proj_kernel,
        out_shape=(
            jax.ShapeDtypeStruct((m, SLAB_W), F32),
            jax.ShapeDtypeStruct((m, SLAB_W), F32),
            jax.ShapeDtypeStruct((m, SLAB_W), F32),
            jax.ShapeDtypeStruct((m, MIX_W), MXU_DTYPE),
            jax.ShapeDtypeStruct((m, MIX_W), MXU_DTYPE),
            jax.ShapeDtypeStruct((m // tm, HEADS * LANES, tm), MXU_DTYPE),
        ),
        grid=(m // tm,),
        in_specs=[
            pl.BlockSpec((tm, d), row),
            pl.BlockSpec((1, d), full),
            pl.BlockSpec((d, SLAB_W), full),
            pl.BlockSpec((d, 3 * MIX_W), full),
            pl.BlockSpec((d, SLAB_W), full),
            pl.BlockSpec((d, SLAB_W), full),
            pl.BlockSpec((tm, LANES), lambda i: (i % tiles_per_seq, 0)),
            pl.BlockSpec((tm, LANES), lambda i: (i % tiles_per_seq, 0)),
        ],
        out_specs=(
            pl.BlockSpec((tm, SLAB_W), row),
            pl.BlockSpec((tm, SLAB_W), row),
            pl.BlockSpec((tm, SLAB_W), row),
            pl.BlockSpec((tm, MIX_W), row),
            pl.BlockSpec((tm, MIX_W), row),
            pl.BlockSpec((1, HEADS * LANES, tm), lambda i: (i, 0, 0)),
        ),
        compiler_params=pltpu.CompilerParams(dimension_semantics=("parallel",), vmem_limit_bytes=VMEM_LIMIT),
        name="mixer_in_proj",
    )(x2, pre_g.reshape(1, d), wa, wb, wc, wd, cos_t, sin_t)


def _attn_kernel(q_ref, k_ref, vt_ref, lam_ref, g_ref, o_ref, qt_ref, m_ref, acc_ref, p_ref, s_ref, alpha_ref,
                 *, tq, tk, lambda_init):
    i = pl.program_id(1)
    q_t = q_ref[0].astype(F32).T
    dim_map = _iota((MIX_W, tq), 0) >> DIFF_DH_SHIFT
    for mp in range(2 * HEADS):
        qt_ref[mp // 2, :, (mp % 2) * tq:(mp % 2 + 1) * tq] = jnp.where(dim_map == mp, q_t, 0.0).astype(qt_ref.dtype)
    m_ref[...] = jnp.full_like(m_ref, NEG)
    acc_ref[...] = jnp.zeros_like(acc_ref)
    qpos = i * tq + (_iota((1, 2 * tq), 1) & (tq - 1))
    limit = ((qpos >> CHUNK_SHIFT) + 1) << CHUNK_SHIFT
    n_kt = (i * tq + tq + tk - 1) // tk

    def step(j, masked):
        start = pl.multiple_of(j * tk, tk)
        kt = k_ref[0, pl.ds(start, tk), :]

        def scores(h):
            s_ref[h] = jnp.dot(kt, qt_ref[h], preferred_element_type=F32)

        def piece(h, r0):
            sp = s_ref[h, r0:r0 + ATT_RB, :]
            if masked:
                key = start + r0 + _iota((ATT_RB, 1), 0)
                sp = jnp.where(key < limit, sp, NEG)
            return sp

        def softmax(h):
            mx = piece(h, 0)
            for r0 in range(ATT_RB, tk, ATT_RB):
                mx = jnp.maximum(mx, piece(h, r0))
            m_old = m_ref[h]
            m_new = jnp.maximum(m_old, jnp.max(mx, axis=0, keepdims=True))
            for r0 in range(0, tk, ATT_RB):
                p_ref[h, r0:r0 + ATT_RB, :] = jnp.exp2(piece(h, r0) - m_new).astype(p_ref.dtype)
            alpha_ref[h] = jnp.exp2(m_old - m_new)
            m_ref[h] = m_new

        def values(h):
            pv = jnp.dot(vt_ref[j, h * LANES:(h + 1) * LANES, :], p_ref[h], preferred_element_type=F32)
            acc_ref[h] = alpha_ref[h] * acc_ref[h] + pv

        scores(0)
        scores(1)
        for h in range(HEADS):
            softmax(h)
            if h + 2 < HEADS:
                scores(h + 2)
            values(h)

    def body(j, carry):
        step(j, False)
        return carry

    lax.fori_loop(0, n_kt - 1, body, 0)
    step(n_kt - 1, True)

    lam_p = lam_ref[...]
    lam = (jnp.exp(jnp.sum(lam_p[0:1] * lam_p[1:2], axis=-1, keepdims=True))
           - jnp.exp(jnp.sum(lam_p[2:3] * lam_p[3:4], axis=-1, keepdims=True)) + lambda_init)
    outs = []
    for h in range(HEADS):
        a = acc_ref[h]
        o = (a[:HEAD_W, :tq] / a[HEAD_W:HEAD_W + 1, :tq]
             - lam * (a[:HEAD_W, tq:] / a[HEAD_W:HEAD_W + 1, tq:]))
        outs.append(o * lax.rsqrt(jnp.mean(o * o, axis=0, keepdims=True) + EPS))
    o_ref[0] = jnp.concatenate(outs, axis=0).T * g_ref[...] * (1.0 - lambda_init)


def _attn_call(dq, dk, dvt, lam_p, norm_g, lambda_init, bsz, seq, tq=ATT_TQ, tk=ATT_TK):
    q3 = dq.reshape(bsz, seq, MIX_W)
    k3 = dk.reshape(bsz, seq, MIX_W)
    n_kt = seq // tk
    out = pl.pallas_call(
        functools.partial(_attn_kernel, tq=tq, tk=tk, lambda_init=lambda_init),
        out_shape=jax.ShapeDtypeStruct((bsz, seq, MIX_W), F32),
        grid=(bsz, seq // tq),
        in_specs=[
            pl.BlockSpec((1, tq, MIX_W), lambda b, i: (b, i, 0)),
            pl.BlockSpec((1, seq, MIX_W), lambda b, i: (b, 0, 0)),
            pl.BlockSpec((n_kt, HEADS * LANES, tk), lambda b, i: (b, 0, 0)),
            pl.BlockSpec((4, DIFF_DH), lambda b, i: (0, 0)),
            pl.BlockSpec((1, MIX_W), lambda b, i: (0, 0)),
        ],
        out_specs=pl.BlockSpec((1, tq, MIX_W), lambda b, i: (b, i, 0)),
        scratch_shapes=[
            pltpu.VMEM((HEADS, MIX_W, 2 * tq), MXU_DTYPE),
            pltpu.VMEM((HEADS, 1, 2 * tq), F32),
            pltpu.VMEM((HEADS, LANES, 2 * tq), F32),
            pltpu.VMEM((HEADS, tk, 2 * tq), MXU_DTYPE),
            pltpu.VMEM((HEADS, tk, 2 * tq), F32),
            pltpu.VMEM((HEADS, 1, 2 * tq), F32),
        ],
        compiler_params=pltpu.CompilerParams(dimension_semantics=("parallel", "arbitrary"),
                                             vmem_limit_bytes=VMEM_LIMIT),
        name="diff_attention",
    )(q3, k3, dvt, lam_p, jnp.tile(norm_g, HEADS).reshape(1, MIX_W))
    return out.reshape(bsz * seq, MIX_W)


def _bd(x, mask):
    x4 = jnp.concatenate([x] * HEADS, axis=0)
    return jnp.where(mask, x4, jnp.zeros_like(x4))


def _split(x):
    hi = x.astype(MXU_DTYPE)
    return hi, (x - hi.astype(F32)).astype(MXU_DTYPE)


def _bd_matmul3(a, b, mask):
    ah, al = _split(a)
    bh, bl = _split(b)
    m = a.shape[0]
    r = jnp.dot(jnp.concatenate([ah, al], axis=0), _bd(bh, mask), preferred_element_type=F32)
    return r[:m] + r[m:] + jnp.dot(ah, _bd(bl, mask), preferred_element_type=F32)


def _chunk_slices(t):
    return [slice(c * CHUNK, (c + 1) * CHUNK) for c in range(t // CHUNK)]


def _ssd_steps(slab_ref, cw_ref, cb_ref, alog_ref, dtb_ref, dskip_ref, g_ref, y_ref,
               carry_ref, buf_ref, st_ref, xc_ref, xcd_ref, bm_ref, cm_ref, lm_ref, ea_ref, el_ref, yo_ref, t):
    ltile, _, pos, mm = _chunk_consts(t)
    expand = _head_expand(0)
    xbc = _silu(_causal_conv(slab_ref[0, :, MIX_W:4 * MIX_W], cw_ref, carry_ref, buf_ref, t) + cb_ref[...])
    x = xbc[:, 0:MIX_W]
    bm_ref[...] = xbc[:, MIX_W:2 * MIX_W].astype(bm_ref.dtype)
    cm_ref[...] = xbc[:, 2 * MIX_W:3 * MIX_W].astype(cm_ref.dtype)
    yield
    dt = _softplus(slab_ref[0, :, 4 * MIX_W:] + dtb_ref[...])
    da = dt * (-jnp.exp(alog_ref[...]))
    dt_e = _dot_xsel(dt, expand)
    da_e = _dot_xsel(da, expand)
    yield
    acs_e = _dot_sel(ltile, da_e)
    acs_last_e = _chunk_last(acs_e)
    ldiff = _dot_sel(ltile, jnp.where(pos > mm, da_e, 0.0))
    lm_ref[...] = jnp.where(mm <= pos, jnp.exp(ldiff), 0.0)
    xc = x * dt_e
    xc_ref[...] = xc.astype(xc_ref.dtype)
    xcd_ref[...] = (xc * jnp.exp(acs_last_e - acs_e)).astype(xcd_ref.dtype)
    ea_ref[...] = jnp.exp(acs_e)
    el_ref[...] = jnp.exp(acs_last_e)
    yield
    for c in range(t // CHUNK):
        r0 = c * CHUNK
        rows = slice(r0, r0 + CHUNK)
        for grp in range(2):
            gl = slice(grp * SSD_N, (grp + 1) * SSD_N)
            bg = bm_ref[rows, gl]
            cg = cm_ref[rows, gl]
            cb = _dot_nt(cg, bg)
            ys = []
            for hh in range(2):
                hl = slice(grp * SSD_N + hh * HEAD_W, grp * SSD_N + (hh + 1) * HEAD_W)
                ys.append(_dot(cb * lm_ref[rows, hl], xc_ref[rows, hl]))
            y_diag = jnp.concatenate(ys, axis=-1)
            st = st_ref[grp]
            y_off = _dot(cg, st) * ea_ref[rows, gl]
            st_ref[grp] = st * el_ref[r0:r0 + 1, gl] + _dot_tn(bg, xcd_ref[rows, gl])
            yo_ref[rows, gl] = y_diag + y_off
        yield
    y = (yo_ref[...] + x * dskip_ref[...]) * _silu(slab_ref[0, :, 0:MIX_W])
    gs = g_ref[...]
    for grp in range(2):
        gl = slice(grp * SSD_N, (grp + 1) * SSD_N)
        y_ref[0, :, gl] = _rms(y[:, gl], gs[:, gl])


def _gdn_steps(slab_ref, cw_ref, alog_ref, dtb_ref, g_ref, y_ref,
               carry_ref, buf_ref, st_ref, qn_ref, kn_ref, qd_ref, kd_ref, kb_ref, vb_ref, kbg_ref, dec_ref,
               egl_ref, u_ref, w_ref, attn_ref, o_ref, t):
    ltile, _, pos, mm = _chunk_consts(t)
    ones_h = _head_ones()
    qkv = _silu(_causal_conv(slab_ref[0, :, 0:3 * MIX_W], cw_ref, carry_ref, buf_ref, t))
    q = qkv[:, 0:MIX_W]
    k = qkv[:, MIX_W:2 * MIX_W]
    v = qkv[:, 2 * MIX_W:]
    yield
    qn = q * lax.rsqrt(_dot_xsel(q * q, ones_h, terms=2) + EPS) * (HEAD_W ** -0.5)
    kn = k * lax.rsqrt(_dot_xsel(k * k, ones_h, terms=2) + EPS)
    yield
    ba = slab_ref[0, :, 4 * MIX_W:]
    beta_e = _dot_xsel(_sigmoid(ba), _head_expand(0))
    g4 = -jnp.exp(alog_ref[...]) * _softplus(ba + dtb_ref[...])
    g_e = _dot_xsel(g4, _head_expand(HEADS))
    yield
    gc_e = _dot_sel(ltile, g_e)
    gl_e = _chunk_last(gc_e)
    diff = _dot_sel(ltile, jnp.where(pos > mm, g_e, 0.0))
    dec_ref[...] = jnp.where(mm <= pos, jnp.exp(diff), 0.0)
    egc = jnp.exp(gc_e)
    kb = kn * beta_e
    qn_ref[...] = qn
    kn_ref[...] = kn
    qd_ref[...] = qn * egc
    kd_ref[...] = kn * jnp.exp(gl_e - gc_e)
    kb_ref[...] = kb
    vb_ref[...] = v * beta_e
    kbg_ref[...] = kb * egc
    egl_ref[...] = jnp.exp(gl_e)
    yield

    head_mask = _head_mask()
    r_ss = _iota((CHUNK, MIX_W), 0)
    m_ss = _iota((CHUNK, MIX_W), 1) & (CHUNK - 1)
    eye_ss = jnp.where(r_ss == m_ss, 1.0, 0.0).astype(F32)
    chunk_rows = _chunk_slices(t)

    ys, tms = [], []
    for rows in chunk_rows:
        lhs = jnp.concatenate([kb_ref[rows, :], qn_ref[rows, :]], axis=0).astype(MXU_DTYPE)
        raw = lax.dot_general(lhs, _bd(kn_ref[rows, :].astype(MXU_DTYPE), head_mask), (((1,), (1,)), ((), ())),
                              preferred_element_type=F32)
        dec = dec_ref[rows, :]
        attn_ref[rows, :] = raw[CHUNK:] * dec
        y = jnp.where(m_ss < r_ss, -(raw[:CHUNK] * dec), 0.0)
        ys.append(y)
        tms.append(eye_ss + y)
    yield
    ys = [_bd_matmul3(y, y, head_mask) for y in ys]
    yield
    for _ in range(4):
        rs = [_bd_matmul3(jnp.concatenate([tm, y], axis=0), y, head_mask) for tm, y in zip(tms, ys)]
        tms = [tm + r[:CHUNK] for tm, r in zip(tms, rs)]
        ys = [r[CHUNK:] for r in rs]
        yield
    tms = [tm + _bd_matmul3(tm, y, head_mask) for tm, y in zip(tms, ys)]
    yield
    for rows, tm in zip(chunk_rows, tms):
        rhs = jnp.concatenate([_bd(vb_ref[rows, :].astype(MXU_DTYPE), head_mask),
                               _bd(kbg_ref[rows, :].astype(MXU_DTYPE), head_mask)], axis=1)
        uw = jnp.dot(tm.astype(MXU_DTYPE), rhs, preferred_element_type=F32)
        u_ref[rows, :] = uw[:, :MIX_W]
        w_ref[rows, :] = uw[:, MIX_W:]
    yield

    st = st_ref[...]
    for c, rows in enumerate(chunk_rows):
        lhs = jnp.concatenate([w_ref[rows, :], qd_ref[rows, :]], axis=0).astype(MXU_DTYPE)
        r = jnp.dot(lhs, st.astype(MXU_DTYPE), preferred_element_type=F32)
        v_new = u_ref[rows, :] - r[:CHUNK]
        v_new_c = v_new.astype(MXU_DTYPE)
        o_ref[rows, :] = r[CHUNK:] + jnp.dot(attn_ref[rows, :].astype(MXU_DTYPE), _bd(v_new_c, head_mask),
                                             preferred_element_type=F32)
        kv = lax.dot_general(kd_ref[rows, :].astype(MXU_DTYPE), v_new_c, (((0,), (0,)), ((), ())),
                             preferred_element_type=F32)
        st = st * egl_ref[c * CHUNK:c * CHUNK + 1, :] + jnp.where(head_mask, kv, 0.0)
        yield
    st_ref[...] = st

    o = o_ref[...]
    o = o * lax.rsqrt(_dot_xsel(o * o, ones_h, terms=2) * (1.0 / HEAD_W) + EPS) * g_ref[...]
    y_ref[0] = o * _silu(slab_ref[0, :, 3 * MIX_W:4 * MIX_W])


def _mlstm_steps(slab_ref, bias_ref, g_ref, y_ref, st_ref, m_ref, dl_ref, b_ref, ge_ref, dmax_ref, qk_ref,
                 o_ref, t):
    ltile, csame, pos, mm = _chunk_consts(t)
    gates = slab_ref[0, :, 4 * MIX_W:] + bias_ref[...]
    li_e = _dot_xsel(gates, _head_expand(0))
    lf_e = _dot_xsel(-_softplus(-gates), _head_expand(HEADS))
    yield
    b_e = _dot_sel(ltile, lf_e)
    dlog = _dot_sel(ltile, jnp.where(pos > mm, lf_e, 0.0)) + _dot_sel(csame, jnp.where(pos == mm, li_e, 0.0))
    dl = jnp.where(mm <= pos, dlog, NEG)
    yield
    lane_head = _iota((t, MIX_W), 1) >> CHUNK_SHIFT
    dmax = jnp.full((t, MIX_W), NEG, F32)
    for h in range(HEADS):
        in_head = lane_head == h
        dmax = jnp.where(in_head, jnp.max(jnp.where(in_head, dl, NEG), axis=-1, keepdims=True), dmax)
    dl_ref[...] = dl
    dmax_ref[...] = dmax
    b_ref[...] = b_e
    ge_ref[...] = _chunk_last(b_e) - b_e + li_e
    yield

    head_mask = _head_mask()
    head_mask2 = jnp.concatenate([head_mask, head_mask], axis=1)
    ones_bd = jnp.where(head_mask, 1.0, 0.0).astype(MXU_DTYPE)
    ones_rows = jnp.ones((CHUNK, MIX_W), MXU_DTYPE)
    chunk_rows = _chunk_slices(t)
    k_scale = HEAD_W ** -0.5
    for rows in chunk_rows:
        k_c = (slab_ref[0, rows, MIX_W:2 * MIX_W] * k_scale).astype(MXU_DTYPE)
        qk_ref[rows, :] = lax.dot_general(slab_ref[0, rows, 0:MIX_W].astype(MXU_DTYPE), _bd(k_c, head_mask),
                                          (((1,), (1,)), ((), ())), preferred_element_type=F32)
    yield

    st = st_ref[...]
    m_st = m_ref[...]
    for rows in chunk_rows:
        q_c = slab_ref[0, rows, 0:MIX_W].astype(MXU_DTYPE)
        k_c = slab_ref[0, rows, MIX_W:2 * MIX_W] * k_scale
        v_c = slab_ref[0, rows, 2 * MIX_W:3 * MIX_W].astype(MXU_DTYPE)
        b_c = b_ref[rows, :]
        m_t = jnp.maximum(b_c + m_st, dmax_ref[rows, :])
        w_st = jnp.exp(b_c + m_st - m_t)
        s = qk_ref[rows, :] * jnp.exp(dl_ref[rows, :] - m_t)
        qs = jnp.dot(q_c, st.astype(MXU_DTYPE), preferred_element_type=F32)
        sv = jnp.dot(s.astype(MXU_DTYPE), jnp.concatenate([_bd(v_c, head_mask), ones_bd], axis=1),
                     preferred_element_type=F32)
        num = w_st * qs[:, :MIX_W] + sv[:, :MIX_W]
        den = w_st * qs[:, MIX_W:] + sv[:, MIX_W:]
        o_ref[rows, :] = num / jnp.maximum(jnp.abs(den), jnp.exp(-m_t))
        ge_c = ge_ref[rows, :]
        b_last = b_c[CHUNK - 1:CHUNK, :]
        m_new = jnp.maximum(b_last + m_st, jnp.max(ge_c, axis=0, keepdims=True))
        w_old = jnp.exp(b_last + m_st - m_new)
        k_w = (k_c * jnp.exp(ge_c - m_new)).astype(MXU_DTYPE)
        upd = lax.dot_general(k_w, jnp.concatenate([v_c, ones_rows], axis=1), (((0,), (0,)), ((), ())),
                              preferred_element_type=F32)
        st = st * jnp.concatenate([w_old, w_old], axis=1) + jnp.where(head_mask2, upd, 0.0)
        m_st = m_new
        yield
    st_ref[...] = st
    m_ref[...] = m_st

    hcat = o_ref[...] * _sigmoid(slab_ref[0, :, 3 * MIX_W:4 * MIX_W])
    ms = _dot_xsel(hcat * hcat, _head_ones(), terms=2) * (1.0 / HEAD_W)
    y_ref[0] = hcat * lax.rsqrt(ms + EPS) * g_ref[...]


_N_GDN_REFS = (5, 1, 16)
_N_SSD_REFS = (7, 1, 11)
_N_MLSTM_REFS = (3, 1, 8)


def _mixers_kernel(*refs, t):
    counts = (_N_GDN_REFS, _N_SSD_REFS, _N_MLSTM_REFS)
    groups = []
    pos = 0
    for kind in range(3):
        per_mixer = []
        for c in counts:
            per_mixer.append(refs[pos:pos + c[kind]])
            pos += c[kind]
        groups.append(per_mixer)
    gdn_refs, ssd_refs, ml_refs = [groups[0][i] + groups[1][i] + groups[2][i] for i in range(3)]

    @pl.when(pl.program_id(1) == 0)
    def _():
        for state_ref in (gdn_refs[6], gdn_refs[8], ssd_refs[8], ssd_refs[10], ml_refs[4], ml_refs[5]):
            state_ref[...] = jnp.zeros_like(state_ref)

    gdn = _gdn_steps(*gdn_refs, t)
    streams = [gdn, gdn, _ssd_steps(*ssd_refs, t), _mlstm_steps(*ml_refs, t)]
    while streams:
        for stream in list(streams):
            if stream in streams and next(stream, StopIteration) is StopIteration:
                streams = [s for s in streams if s is not stream]


def _mixers_call(gdn_slab, ssd_slab, ml_slab, p, bsz, seq, t=MIX_T):
    ch = 3 * MIX_W
    lane_pad = lambda v, before: jnp.pad(v, (before, LANES - before - v.shape[0])).reshape(1, LANES)
    full = lambda b, i: (0, 0)
    slab_spec = pl.BlockSpec((1, t, SLAB_W), lambda b, i: (b, i, 0))
    vec = lambda n: pl.BlockSpec((1, n), full)
    tile = lambda dtype=F32: pltpu.VMEM((t, MIX_W), dtype)
    out_spec = pl.BlockSpec((1, t, MIX_W), lambda b, i: (b, i, 0))
    out_shape = jax.ShapeDtypeStruct((bsz, seq, MIX_W), F32)
    conv_scratch = [pltpu.VMEM((8, ch), F32), pltpu.VMEM((t + 8, ch), F32)]
    gdn_in = [gdn_slab.reshape(bsz, seq, SLAB_W), p["gdn_conv_w"], lane_pad(p["gdn_a_log"], HEADS),
              lane_pad(p["gdn_dt_bias"], HEADS), jnp.tile(p["gdn_norm_g"], HEADS).reshape(1, MIX_W)]
    gdn_specs = [slab_spec, pl.BlockSpec((CONV_K, ch), full), vec(LANES), vec(LANES), vec(MIX_W)]
    gdn_scratch = conv_scratch + [pltpu.VMEM((MIX_W, MIX_W), F32)] + [tile() for _ in range(13)]
    ssd_in = [ssd_slab.reshape(bsz, seq, SLAB_W), p["ssd_conv_w"], p["ssd_conv_b"].reshape(1, ch),
              lane_pad(p["ssd_a_log"], 0), lane_pad(p["ssd_dt_bias"], 0),
              jnp.repeat(p["ssd_d"], HEAD_W).reshape(1, MIX_W), p["ssd_norm_g"].reshape(1, MIX_W)]
    ssd_specs = [slab_spec, pl.BlockSpec((CONV_K, ch), full), vec(ch), vec(LANES), vec(LANES), vec(MIX_W), vec(MIX_W)]
    ssd_scratch = (conv_scratch + [pltpu.VMEM((2, SSD_N, SSD_N), F32)] + [tile(MXU_DTYPE) for _ in range(4)]
                   + [tile() for _ in range(4)])
    ml_in = [ml_slab.reshape(bsz, seq, SLAB_W),
             lane_pad(jnp.concatenate([p["mlstm_i_bias"], p["mlstm_f_bias"]]), 0), p["mlstm_norm_g"].reshape(1, MIX_W)]
    ml_specs = [slab_spec, vec(LANES), vec(MIX_W)]
    ml_scratch = [pltpu.VMEM((MIX_W, 2 * MIX_W), F32), pltpu.VMEM((1, MIX_W), F32)] + [tile() for _ in range(6)]
    assert (len(gdn_in), 1, len(gdn_scratch)) == _N_GDN_REFS
    assert (len(ssd_in), 1, len(ssd_scratch)) == _N_SSD_REFS
    assert (len(ml_in), 1, len(ml_scratch)) == _N_MLSTM_REFS
    outs = pl.pallas_call(
        functools.partial(_mixers_kernel, t=t),
        out_shape=(out_shape, out_shape, out_shape),
        grid=(bsz, seq // t),
        in_specs=gdn_specs + ssd_specs + ml_specs,
        out_specs=(out_spec, out_spec, out_spec),
        scratch_shapes=gdn_scratch + ssd_scratch + ml_scratch,
        compiler_params=pltpu.CompilerParams(dimension_semantics=("parallel", "arbitrary"),
                                             vmem_limit_bytes=VMEM_LIMIT),
        name="recurrent_mixers",
    )(*gdn_in, *ssd_in, *ml_in)
    return [o.reshape(bsz * seq, MIX_W) for o in outs]


def _rope_tables(seq):
    inv_freq = ROPE_THETA ** (-jnp.arange(0, DIFF_DH, 2, dtype=F32) / DIFF_DH)
    ang = jnp.arange(seq, dtype=F32)[:, None] * inv_freq[None, :]
    cos, sin = jnp.cos(ang), jnp.sin(ang)
    reps = LANES // DIFF_DH
    cos_t = jnp.tile(jnp.concatenate([cos, cos], axis=-1), (1, reps))
    sin_t = jnp.tile(jnp.concatenate([-sin, sin], axis=-1), (1, reps))
    return cos_t, sin_t


def kernel(x, ffn1_pre_g, ffn1_w_gate, ffn1_w_up, ffn1_w_down, ffn1_post_g, mix_pre_g, w_in, gdn_conv_w, gdn_a_log, gdn_dt_bias, gdn_norm_g, diff_lam_q1, diff_lam_k1, diff_lam_q2, diff_lam_k2, diff_norm_g, ssd_conv_w, ssd_conv_b, ssd_a_log, ssd_dt_bias, ssd_d, ssd_norm_g, mlstm_i_bias, mlstm_f_bias, mlstm_norm_g, w_out, mix_post_g, ffn2_pre_g, ffn2_w_gate, ffn2_w_up, ffn2_w_down, ffn2_post_g):
    bsz, seq, d = x.shape
    x2 = x.reshape(bsz * seq, d)
    cos_t, sin_t = _rope_tables(seq)
    for l in range(DEPTH):
        lambda_init = 0.8 - 0.6 * math.exp(-0.3 * l)
        x2 = _ffn_call(x2, ffn1_pre_g[l], ffn1_w_gate[l], ffn1_w_up[l], ffn1_w_down[l], ffn1_post_g[l])
        gdn_slab, ssd_slab, ml_slab, dq, dk, dv = _proj_call(x2, mix_pre_g[l], w_in[l], cos_t, sin_t, seq)
        mixer_params = dict(
            gdn_conv_w=gdn_conv_w[l], gdn_a_log=gdn_a_log[l], gdn_dt_bias=gdn_dt_bias[l], gdn_norm_g=gdn_norm_g[l],
            ssd_conv_w=ssd_conv_w[l], ssd_conv_b=ssd_conv_b[l], ssd_a_log=ssd_a_log[l], ssd_dt_bias=ssd_dt_bias[l],
            ssd_d=ssd_d[l], ssd_norm_g=ssd_norm_g[l],
            mlstm_i_bias=mlstm_i_bias[l], mlstm_f_bias=mlstm_f_bias[l], mlstm_norm_g=mlstm_norm_g[l])
        out_a, out_c, out_d = _mixers_call(gdn_slab, ssd_slab, ml_slab, mixer_params, bsz, seq)
        lam_p = jnp.stack([diff_lam_q1[l], diff_lam_k1[l], diff_lam_q2[l], diff_lam_k2[l]])
        out_b = _attn_call(dq, dk, dv, lam_p, diff_norm_g[l], lambda_init, bsz, seq)
        x2 = _ffn_call(x2, ffn2_pre_g[l], ffn2_w_gate[l], ffn2_w_up[l], ffn2_w_down[l], ffn2_post_g[l],
                       mix=((out_a, out_b, out_c, out_d), w_out[l], mix_post_g[l]))
    return x2.reshape(bsz, seq, d)
```

```python
import functools
import math

import jax
import jax.numpy as jnp
from jax import lax
from jax.experimental import pallas as pl
from jax.experimental.pallas import tpu as pltpu

F32 = jnp.float32
MXU_DTYPE = jnp.bfloat16

D_MODEL = 1024
D_FF = 2816
DEPTH = 2
CHUNK = 64
CHUNK_SHIFT = 6
CONV_K = 4
EPS = 1e-6
ROPE_THETA = 10000.0
HEADS = 4
HEAD_W = 64
MIX_W = HEADS * HEAD_W
DIFF_DH = 32
DIFF_DH_SHIFT = 5
LOG2_E = math.log2(math.e)
SSD_N = 128
LANES = 128
NEG = -1e30

_GDN_COLS = (0, 1032)
_DIFF_COLS = (1032, 1800)
_SSD_COLS = (1800, 2828)
_MLSTM_COLS = (2828, 3860)
SLAB_W = 1152

FFN_TM = 512
FFN_FC = 256
MIX_T = 256
ATT_TQ = 256
ATT_RB = 16
ATT_TK = FFN_TM
VMEM_LIMIT = 56 * 1024 * 1024


def _dot(a, b):
    return jnp.dot(a.astype(MXU_DTYPE), b.astype(MXU_DTYPE), preferred_element_type=F32)


def _dot_nt(a, b):
    return lax.dot_general(a.astype(MXU_DTYPE), b.astype(MXU_DTYPE), (((1,), (1,)), ((), ())),
                           preferred_element_type=F32)


def _dot_tn(a, b):
    return lax.dot_general(a.astype(MXU_DTYPE), b.astype(MXU_DTYPE), (((0,), (0,)), ((), ())),
                           preferred_element_type=F32)


def _split3(x):
    x1 = x.astype(MXU_DTYPE)
    r1 = x - x1.astype(F32)
    x2 = r1.astype(MXU_DTYPE)
    return x1, x2, (r1 - x2.astype(F32)).astype(MXU_DTYPE)


def _dot_sel(sel, x, terms=3):
    sel = sel.astype(MXU_DTYPE)
    return sum(jnp.dot(sel, xi, preferred_element_type=F32) for xi in _split3(x)[:terms])


def _dot_xsel(x, sel, terms=3):
    sel = sel.astype(MXU_DTYPE)
    return sum(jnp.dot(xi, sel, preferred_element_type=F32) for xi in _split3(x)[:terms])


def _chunk_last(x):
    t, w = x.shape
    return jnp.concatenate([jnp.broadcast_to(x[c + CHUNK - 1:c + CHUNK, :], (CHUNK, w))
                            for c in range(0, t, CHUNK)], axis=0)


def _rms(x, g):
    return x * lax.rsqrt(jnp.mean(x * x, axis=-1, keepdims=True) + EPS) * g


def _silu(x):
    return x / (1.0 + jnp.exp(-x))


def _sigmoid(x):
    return 1.0 / (1.0 + jnp.exp(-x))


def _softplus(x):
    return jnp.maximum(x, 0.0) + jnp.log(1.0 + jnp.exp(-jnp.abs(x)))


def _iota(shape, dim):
    return lax.broadcasted_iota(jnp.int32, shape, dim)


def _chunk_consts(t):
    r = _iota((t, t), 0)
    c = _iota((t, t), 1)
    same = (r >> CHUNK_SHIFT) == (c >> CHUNK_SHIFT)
    ltile = jnp.where(same & (c <= r), 1.0, 0.0).astype(F32)
    csame = jnp.where(same, 1.0, 0.0).astype(F32)
    pos = _iota((t, MIX_W), 0) & (CHUNK - 1)
    m = _iota((t, MIX_W), 1) & (CHUNK - 1)
    return ltile, csame, pos, m


def _head_expand(first_lane):
    r = _iota((LANES, MIX_W), 0)
    c = _iota((LANES, MIX_W), 1)
    return jnp.where(r == (c >> CHUNK_SHIFT) + first_lane, 1.0, 0.0).astype(F32)


def _head_mask():
    return (_iota((MIX_W, MIX_W), 0) >> CHUNK_SHIFT) == (_iota((MIX_W, MIX_W), 1) >> CHUNK_SHIFT)


def _head_ones():
    return jnp.where(_head_mask(), 1.0, 0.0).astype(F32)


def _causal_conv(u, w_ref, carry_ref, buf_ref, t):
    buf_ref[0:8, :] = carry_ref[...]
    buf_ref[8:8 + t, :] = u
    carry_ref[...] = u[t - 8:t, :]
    acc = w_ref[CONV_K - 1:CONV_K, :] * u
    for j in range(CONV_K - 1):
        off = 8 - (CONV_K - 1) + j
        acc = acc + w_ref[j:j + 1, :] * buf_ref[off:off + t, :]
    return acc


def _ffn_body(x, pre_ref, wg_ref, wu_ref, wd_ref, post_ref, h_ref, acc_ref, fc):
    h_ref[...] = _rms(x, pre_ref[...]).astype(h_ref.dtype)
    acc_ref[...] = jnp.zeros_like(acc_ref)
    for c0 in range(0, wg_ref.shape[1], fc):
        g = jnp.dot(h_ref[...], wg_ref[:, c0:c0 + fc], preferred_element_type=F32)
        u = jnp.dot(h_ref[...], wu_ref[:, c0:c0 + fc], preferred_element_type=F32)
        a = (_silu(g) * u).astype(wd_ref.dtype)
        acc_ref[...] += jnp.dot(a, wd_ref[c0:c0 + fc, :], preferred_element_type=F32)
    return x + 0.5 * _rms(acc_ref[...], post_ref[...])


def _ffn_kernel(x_ref, pre_ref, wg_ref, wu_ref, wd_ref, post_ref, o_ref, h_ref, acc_ref, *, fc):
    o_ref[...] = _ffn_body(x_ref[...], pre_ref, wg_ref, wu_ref, wd_ref, post_ref, h_ref, acc_ref, fc)


def _outproj_ffn_kernel(x_ref, a_ref, b_ref, c_ref, d_ref, wo_ref, go_ref, pre_ref, wg_ref, wu_ref, wd_ref, post_ref,
                        o_ref, h_ref, acc_ref, *, fc):
    mixed = jnp.concatenate([a_ref[...], b_ref[...], c_ref[...], d_ref[...]], axis=-1).astype(wo_ref.dtype)
    x = x_ref[...] + _rms(jnp.dot(mixed, wo_ref[...], preferred_element_type=F32), go_ref[...])
    o_ref[...] = _ffn_body(x, pre_ref, wg_ref, wu_ref, wd_ref, post_ref, h_ref, acc_ref, fc)


def _resident(shape):
    return pl.BlockSpec(shape, lambda i: (0,) * len(shape), pipeline_mode=pl.Buffered(1))


def _ffn_call(x2, pre_g, w_gate, w_up, w_down, post_g, mix=None, tm=FFN_TM, fc=FFN_FC):
    m, d = x2.shape
    ff = w_gate.shape[1]
    row = lambda i: (i, 0)
    ffn_specs = [_resident((1, d)), _resident((d, ff)), _resident((d, ff)), _resident((ff, d)), _resident((1, d))]
    ffn_args = (pre_g.reshape(1, d), w_gate.astype(MXU_DTYPE), w_up.astype(MXU_DTYPE), w_down.astype(MXU_DTYPE),
                post_g.reshape(1, d))
    if mix is None:
        body, name = _ffn_kernel, "ffn"
        specs, args = ffn_specs, ffn_args
    else:
        outs, w_out, mix_post_g = mix
        body, name = _outproj_ffn_kernel, "mixer_out_proj_ffn"
        specs = [pl.BlockSpec((tm, MIX_W), row)] * 4 + [_resident((d, d)), _resident((1, d))] + ffn_specs
        args = (*outs, w_out.astype(MXU_DTYPE), mix_post_g.reshape(1, d)) + ffn_args
    return pl.pallas_call(
        functools.partial(body, fc=fc),
        out_shape=jax.ShapeDtypeStruct((m, d), F32),
        grid=(m // tm,),
        in_specs=[pl.BlockSpec((tm, d), row)] + specs,
        out_specs=pl.BlockSpec((tm, d), row),
        scratch_shapes=[pltpu.VMEM((tm, d), MXU_DTYPE), pltpu.VMEM((tm, d), F32)],
        compiler_params=pltpu.CompilerParams(dimension_semantics=("parallel",), vmem_limit_bytes=VMEM_LIMIT),
        name=name,
    )(x2, *args)


def _rope_half(x, cos, sin):
    lane = _iota(x.shape, 1)
    half = DIFF_DH // 2
    partner = jnp.where((lane & (DIFF_DH - 1)) < half, pltpu.roll(x, LANES - half, 1), pltpu.roll(x, half, 1))
    return x * cos + partner * sin


def _proj_kernel(x_ref, g_ref, wa_ref, wb_ref, wc_ref, wd_ref, cos_ref, sin_ref,
                 gdn_ref, ssd_ref, ml_ref, dq_ref, dk_ref, dv_ref):
    h = _rms(x_ref[...], g_ref[...]).astype(wa_ref.dtype)
    gdn_ref[...] = jnp.dot(h, wa_ref[...], preferred_element_type=F32)
    ssd_ref[...] = jnp.dot(h, wc_ref[...], preferred_element_type=F32)
    ml_ref[...] = jnp.dot(h, wd_ref[...], preferred_element_type=F32)
    d = jnp.dot(h, wb_ref[...], preferred_element_type=F32)
    cos = cos_ref[...]
    sin = sin_ref[...]
    scale = DIFF_DH ** -0.5 * LOG2_E
    for half in range(2):
        lo = half * LANES
        dq_ref[:, lo:lo + LANES] = (_rope_half(d[:, lo:lo + LANES], cos, sin) * scale).astype(dq_ref.dtype)
        dk_ref[:, lo:lo + LANES] = _rope_half(d[:, MIX_W + lo:MIX_W + lo + LANES], cos, sin).astype(dk_ref.dtype)
    ones = jnp.ones((d.shape[0], HEAD_W), F32)
    for h in range(HEADS):
        v_h = d[:, 2 * MIX_W + h * HEAD_W:2 * MIX_W + (h + 1) * HEAD_W]
        dv_ref[0, h * LANES:(h + 1) * LANES, :] = jnp.concatenate([v_h, ones], axis=-1).T.astype(dv_ref.dtype)


def _pad_cols(w, width):
    return jnp.pad(w, ((0, 0), (0, width - w.shape[1])))


def _proj_call(x2, pre_g, w_in, cos_t, sin_t, seq, tm=FFN_TM):
    m, d = x2.shape
    wa = _pad_cols(w_in[:, _GDN_COLS[0]:_GDN_COLS[1]], SLAB_W).astype(MXU_DTYPE)
    wb = w_in[:, _DIFF_COLS[0]:_DIFF_COLS[1]].astype(MXU_DTYPE)
    wc = _pad_cols(w_in[:, _SSD_COLS[0]:_SSD_COLS[1]], SLAB_W).astype(MXU_DTYPE)
    wd = _pad_cols(w_in[:, _MLSTM_COLS[0]:_MLSTM_COLS[1]], SLAB_W).astype(MXU_DTYPE)
    tiles_per_seq = seq // tm
    full = lambda i: (0, 0)
    row = lambda i: (i, 0)
    return pl.pallas_call(
        _proj_kernel,
        out_shape=(
            jax.ShapeDtypeStruct((m, SLAB_W), F32),
            jax.ShapeDtypeStruct((m, SLAB_W), F32),
            jax.ShapeDtypeStruct((m, SLAB_W), F32),
            jax.ShapeDtypeStruct((m, MIX_W), MXU_DTYPE),
            jax.ShapeDtypeStruct((m, MIX_W), MXU_DTYPE),
            jax.ShapeDtypeStruct((m // tm, HEADS * LANES, tm), MXU_DTYPE),
        ),
        grid=(m // tm,),
        in_specs=[
            pl.BlockSpec((tm, d), row),
            pl.BlockSpec((1, d), full),
            pl.BlockSpec((d, SLAB_W), full),
            pl.BlockSpec((d, 3 * MIX_W), full),
            pl.BlockSpec((d, SLAB_W), full),
            pl.BlockSpec((d, SLAB_W), full),
            pl.BlockSpec((tm, LANES), lambda i: (i % tiles_per_seq, 0)),
            pl.BlockSpec((tm, LANES), lambda i: (i % tiles_per_seq, 0)),
        ],
        out_specs=(
            pl.BlockSpec((tm, SLAB_W), row),
            pl.BlockSpec((tm, SLAB_W), row),
            pl.BlockSpec((tm, SLAB_W), row),
            pl.BlockSpec((tm, MIX_W), row),
            pl.BlockSpec((tm, MIX_W), row),
            pl.BlockSpec((1, HEADS * LANES, tm), lambda i: (i, 0, 0)),
        ),
        compiler_params=pltpu.CompilerParams(dimension_semantics=("parallel",), vmem_limit_bytes=VMEM_LIMIT),
        name="mixer_in_proj",
    )(x2, pre_g.reshape(1, d), wa, wb, wc, wd, cos_t, sin_t)


def _attn_kernel(q_ref, k_ref, vt_ref, lam_ref, g_ref, o_ref, qt_ref, m_ref, acc_ref, p_ref, s_ref, alpha_ref,
                 *, tq, tk, lambda_init):
    i = pl.program_id(1)
    @pl.when(i == 0)
    def _():
        qt_ref[...] = jnp.zeros_like(qt_ref)

    q_t = q_ref[0].astype(F32).T
    for mp in range(2 * HEADS):
        dims = slice(mp * DIFF_DH, (mp + 1) * DIFF_DH)
        qt_ref[mp // 2, dims, (mp % 2) * tq:(mp % 2 + 1) * tq] = q_t[dims, :].astype(qt_ref.dtype)
    m_ref[...] = jnp.full_like(m_ref, NEG)
    acc_ref[...] = jnp.zeros_like(acc_ref)
    qpos = i * tq + (_iota((1, 2 * tq), 1) & (tq - 1))
    limit = ((qpos >> CHUNK_SHIFT) + 1) << CHUNK_SHIFT
    n_kt = (i * tq + tq + tk - 1) // tk

    def scores(h, j, rows=tk):
        kt = k_ref[0, pl.ds(pl.multiple_of(j * tk, tk), rows), :]
        s_ref[h, 0:rows, :] = jnp.dot(kt, qt_ref[h], preferred_element_type=F32)

    def softmax(h, j, rows, mask_from):
        def piece(r0):
            sp = s_ref[h, r0:r0 + ATT_RB, :]
            if mask_from is not None and r0 >= mask_from:
                key = j * tk + r0 + _iota((ATT_RB, 1), 0)
                sp = jnp.where(key < limit, sp, NEG)
            return sp

        mx = piece(0)
        for r0 in range(ATT_RB, rows, ATT_RB):
            mx = jnp.maximum(mx, piece(r0))
        m_old = m_ref[h]
        m_new = jnp.maximum(m_old, jnp.max(mx, axis=0, keepdims=True))
        for r0 in range(0, rows, ATT_RB):
            p_ref[h, r0:r0 + ATT_RB, :] = jnp.exp2(piece(r0) - m_new).astype(p_ref.dtype)
        alpha_ref[h] = jnp.exp2(m_old - m_new)
        m_ref[h] = m_new

    def values(h, j, rows):
        pv = jnp.dot(vt_ref[j, h * LANES:(h + 1) * LANES, 0:rows], p_ref[h, 0:rows, :],
                     preferred_element_type=F32)
        acc_ref[h] = alpha_ref[h] * acc_ref[h] + pv

    def step(j, rows, mask_from, prefetch_scores):
        for h in range(HEADS):
            softmax(h, j, rows, mask_from)
            if h + 2 < HEADS:
                scores(h + 2, j, rows)
            elif prefetch_scores:
                scores(h + 2 - HEADS, j + 1)
            values(h, j, rows)

    def body(j, carry):
        step(j, tk, None, True)
        return carry

    scores(0, 0)
    scores(1, 0)
    lax.fori_loop(0, n_kt - 1, body, 0)
    last = n_kt - 1
    diagonal_first = (i * tq) == last * tk

    @pl.when(diagonal_first)
    def _():
        step(last, tq, 0, False)

    @pl.when(jnp.logical_not(diagonal_first))
    def _():
        step(last, tk, tq, False)

    lam_p = lam_ref[...]
    lam = (jnp.exp(jnp.sum(lam_p[0:1] * lam_p[1:2], axis=-1, keepdims=True))
           - jnp.exp(jnp.sum(lam_p[2:3] * lam_p[3:4], axis=-1, keepdims=True)) + lambda_init)
    outs = []
    for h in range(HEADS):
        a = acc_ref[h]
        o = (a[:HEAD_W, :tq] / a[HEAD_W:HEAD_W + 1, :tq]
             - lam * (a[:HEAD_W, tq:] / a[HEAD_W:HEAD_W + 1, tq:]))
        outs.append(o * lax.rsqrt(jnp.mean(o * o, axis=0, keepdims=True) + EPS))
    o_ref[0] = jnp.concatenate(outs, axis=0).T * g_ref[...] * (1.0 - lambda_init)


def _attn_call(dq, dk, dvt, lam_p, norm_g, lambda_init, bsz, seq, tq=ATT_TQ, tk=ATT_TK):
    assert tk == 2 * tq, "the last-key-tile split in _attn_kernel assumes tk == 2 tq"
    q3 = dq.reshape(bsz, seq, MIX_W)
    k3 = dk.reshape(bsz, seq, MIX_W)
    n_kt = seq // tk
    out = pl.pallas_call(
        functools.partial(_attn_kernel, tq=tq, tk=tk, lambda_init=lambda_init),
        out_shape=jax.ShapeDtypeStruct((bsz, seq, MIX_W), F32),
        grid=(bsz, seq // tq),
        in_specs=[
            pl.BlockSpec((1, tq, MIX_W), lambda b, i: (b, i, 0)),
            pl.BlockSpec((1, seq, MIX_W), lambda b, i: (b, 0, 0)),
            pl.BlockSpec((n_kt, HEADS * LANES, tk), lambda b, i: (b, 0, 0)),
            pl.BlockSpec((4, DIFF_DH), lambda b, i: (0, 0)),
            pl.BlockSpec((1, MIX_W), lambda b, i: (0, 0)),
        ],
        out_specs=pl.BlockSpec((1, tq, MIX_W), lambda b, i: (b, i, 0)),
        scratch_shapes=[
            pltpu.VMEM((HEADS, MIX_W, 2 * tq), MXU_DTYPE),
            pltpu.VMEM((HEADS, 1, 2 * tq), F32),
            pltpu.VMEM((HEADS, LANES, 2 * tq), F32),
            pltpu.VMEM((HEADS, tk, 2 * tq), MXU_DTYPE),
            pltpu.VMEM((HEADS, tk, 2 * tq), F32),
            pltpu.VMEM((HEADS, 1, 2 * tq), F32),
        ],
        compiler_params=pltpu.CompilerParams(dimension_semantics=("parallel", "arbitrary"),
                                             vmem_limit_bytes=VMEM_LIMIT),
        name="diff_attention",
    )(q3, k3, dvt, lam_p, jnp.tile(norm_g, HEADS).reshape(1, MIX_W))
    return out.reshape(bsz * seq, MIX_W)


def _head_block(x, h):
    blk = x[:, (h // 2) * LANES:(h // 2 + 1) * LANES]
    own = (_iota(blk.shape, 1) >> CHUNK_SHIFT) == (h % 2)
    return jnp.where(own, blk, jnp.zeros_like(blk))


def _bd_rows(blocks):
    zero = jnp.zeros_like(blocks[0])
    return jnp.concatenate([jnp.concatenate([b, zero] if h < 2 else [zero, b], axis=1)
                            for h, b in enumerate(blocks)], axis=0)


def _bd(x):
    return _bd_rows([_head_block(x, h) for h in range(HEADS)])


def _bd_keep(m):
    return _bd_rows([_head_block(m[h * CHUNK:(h + 1) * CHUNK, :], h) for h in range(HEADS)])


def _split(x):
    hi = x.astype(MXU_DTYPE)
    return hi, (x - hi.astype(F32)).astype(MXU_DTYPE)


def _bd_matmul3(a, b):
    ah, al = _split(a)
    bh, bl = _split(b)
    m = a.shape[0]
    r = jnp.dot(jnp.concatenate([ah, al], axis=0), _bd(bh), preferred_element_type=F32)
    return r[:m] + r[m:] + jnp.dot(ah, _bd(bl), preferred_element_type=F32)


def _chunk_slices(t):
    return [slice(c * CHUNK, (c + 1) * CHUNK) for c in range(t // CHUNK)]


def _ssd_steps(slab_ref, cw_ref, cb_ref, alog_ref, dtb_ref, dskip_ref, g_ref, y_ref,
               carry_ref, buf_ref, st_ref, xc_ref, xcd_ref, bm_ref, cm_ref, lm_ref, ea_ref, el_ref, yo_ref, t):
    ltile, _, pos, mm = _chunk_consts(t)
    expand = _head_expand(0)
    xbc = _silu(_causal_conv(slab_ref[0, :, MIX_W:4 * MIX_W], cw_ref, carry_ref, buf_ref, t) + cb_ref[...])
    x = xbc[:, 0:MIX_W]
    bm_ref[...] = xbc[:, MIX_W:2 * MIX_W].astype(bm_ref.dtype)
    cm_ref[...] = xbc[:, 2 * MIX_W:3 * MIX_W].astype(cm_ref.dtype)
    yield
    dt = _softplus(slab_ref[0, :, 4 * MIX_W:] + dtb_ref[...])
    da = dt * (-jnp.exp(alog_ref[...]))
    dt_e = _dot_xsel(dt, expand)
    da_e = _dot_xsel(da, expand)
    yield
    acs_e = _dot_sel(ltile, da_e)
    acs_last_e = _chunk_last(acs_e)
    ldiff = _dot_sel(ltile, jnp.where(pos > mm, da_e, 0.0))
    lm_ref[...] = jnp.where(mm <= pos, jnp.exp(ldiff), 0.0)
    xc = x * dt_e
    xc_ref[...] = xc.astype(xc_ref.dtype)
    xcd_ref[...] = (xc * jnp.exp(acs_last_e - acs_e)).astype(xcd_ref.dtype)
    ea_ref[...] = jnp.exp(acs_e)
    el_ref[...] = jnp.exp(acs_last_e)
    yield
    for c in range(t // CHUNK):
        r0 = c * CHUNK
        rows = slice(r0, r0 + CHUNK)
        for grp in range(2):
            gl = slice(grp * SSD_N, (grp + 1) * SSD_N)
            bg = bm_ref[rows, gl]
            cg = cm_ref[rows, gl]
            cb = _dot_nt(cg, bg)
            ys = []
            for hh in range(2):
                hl = slice(grp * SSD_N + hh * HEAD_W, grp * SSD_N + (hh + 1) * HEAD_W)
                ys.append(_dot(cb * lm_ref[rows, hl], xc_ref[rows, hl]))
            y_diag = jnp.concatenate(ys, axis=-1)
            st = st_ref[grp]
            y_off = _dot(cg, st) * ea_ref[rows, gl]
            st_ref[grp] = st * el_ref[r0:r0 + 1, gl] + _dot_tn(bg, xcd_ref[rows, gl])
            yo_ref[rows, gl] = y_diag + y_off
        yield
    y = (yo_ref[...] + x * dskip_ref[...]) * _silu(slab_ref[0, :, 0:MIX_W])
    gs = g_ref[...]
    for grp in range(2):
        gl = slice(grp * SSD_N, (grp + 1) * SSD_N)
        y_ref[0, :, gl] = _rms(y[:, gl], gs[:, gl])


def _gdn_steps(slab_ref, cw_ref, alog_ref, dtb_ref, g_ref, y_ref,
               carry_ref, buf_ref, st_ref, qn_ref, kn_ref, qd_ref, kd_ref, kb_ref, vb_ref, kbg_ref, dec_ref,
               egl_ref, u_ref, w_ref, attn_ref, o_ref, t):
    ltile, _, pos, mm = _chunk_consts(t)
    ones_h = _head_ones()
    qkv = _silu(_causal_conv(slab_ref[0, :, 0:3 * MIX_W], cw_ref, carry_ref, buf_ref, t))
    q = qkv[:, 0:MIX_W]
    k = qkv[:, MIX_W:2 * MIX_W]
    v = qkv[:, 2 * MIX_W:]
    yield
    qn = q * lax.rsqrt(_dot_xsel(q * q, ones_h, terms=2) + EPS) * (HEAD_W ** -0.5)
    kn = k * lax.rsqrt(_dot_xsel(k * k, ones_h, terms=2) + EPS)
    yield
    ba = slab_ref[0, :, 4 * MIX_W:]
    beta_e = _dot_xsel(_sigmoid(ba), _head_expand(0))
    g4 = -jnp.exp(alog_ref[...]) * _softplus(ba + dtb_ref[...])
    g_e = _dot_xsel(g4, _head_expand(HEADS))
    yield
    gc_e = _dot_sel(ltile, g_e)
    gl_e = _chunk_last(gc_e)
    diff = _dot_sel(ltile, jnp.where(pos > mm, g_e, 0.0))
    dec_ref[...] = jnp.where(mm <= pos, jnp.exp(diff), 0.0)
    egc = jnp.exp(gc_e)
    kb = kn * beta_e
    qn_ref[...] = qn
    kn_ref[...] = kn
    qd_ref[...] = qn * egc
    kd_ref[...] = kn * jnp.exp(gl_e - gc_e)
    kb_ref[...] = kb
    vb_ref[...] = v * beta_e
    kbg_ref[...] = kb * egc
    egl_ref[...] = jnp.exp(gl_e)
    yield

    r_ss = _iota((CHUNK, MIX_W), 0)
    m_ss = _iota((CHUNK, MIX_W), 1) & (CHUNK - 1)
    eye_ss = jnp.where(r_ss == m_ss, 1.0, 0.0).astype(F32)
    chunk_rows = _chunk_slices(t)

    ys, tms = [], []
    for rows in chunk_rows:
        lhs = jnp.concatenate([kb_ref[rows, :], qn_ref[rows, :]], axis=0).astype(MXU_DTYPE)
        raw = lax.dot_general(lhs, _bd(kn_ref[rows, :].astype(MXU_DTYPE)), (((1,), (1,)), ((), ())),
                              preferred_element_type=F32)
        dec = dec_ref[rows, :]
        attn_ref[rows, :] = raw[CHUNK:] * dec
        y = jnp.where(m_ss < r_ss, -(raw[:CHUNK] * dec), 0.0)
        ys.append(y)
        tms.append(eye_ss + y)
    yield
    ys = [_bd_matmul3(y, y) for y in ys]
    yield
    for _ in range(4):
        rs = [_bd_matmul3(jnp.concatenate([tm, y], axis=0), y) for tm, y in zip(tms, ys)]
        tms = [tm + r[:CHUNK] for tm, r in zip(tms, rs)]
        ys = [r[CHUNK:] for r in rs]
        yield
    tms = [tm + _bd_matmul3(tm, y) for tm, y in zip(tms, ys)]
    yield
    for rows, tm in zip(chunk_rows, tms):
        rhs = jnp.concatenate([_bd(vb_ref[rows, :].astype(MXU_DTYPE)),
                               _bd(kbg_ref[rows, :].astype(MXU_DTYPE))], axis=1)
        uw = jnp.dot(tm.astype(MXU_DTYPE), rhs, preferred_element_type=F32)
        u_ref[rows, :] = uw[:, :MIX_W]
        w_ref[rows, :] = uw[:, MIX_W:]
    yield

    st = st_ref[...]
    for c, rows in enumerate(chunk_rows):
        lhs = jnp.concatenate([w_ref[rows, :], qd_ref[rows, :]], axis=0).astype(MXU_DTYPE)
        r = jnp.dot(lhs, st.astype(MXU_DTYPE), preferred_element_type=F32)
        v_new = u_ref[rows, :] - r[:CHUNK]
        v_new_c = v_new.astype(MXU_DTYPE)
        o_ref[rows, :] = r[CHUNK:] + jnp.dot(attn_ref[rows, :].astype(MXU_DTYPE), _bd(v_new_c),
                                             preferred_element_type=F32)
        kv = lax.dot_general(kd_ref[rows, :].astype(MXU_DTYPE), v_new_c, (((0,), (0,)), ((), ())),
                             preferred_element_type=F32)
        st = st * egl_ref[c * CHUNK:c * CHUNK + 1, :] + _bd_keep(kv)
        yield
    st_ref[...] = st

    o = o_ref[...]
    o = o * lax.rsqrt(_dot_xsel(o * o, ones_h, terms=2) * (1.0 / HEAD_W) + EPS) * g_ref[...]
    y_ref[0] = o * _silu(slab_ref[0, :, 3 * MIX_W:4 * MIX_W])


def _mlstm_steps(slab_ref, bias_ref, g_ref, y_ref, st_ref, m_ref, dl_ref, b_ref, ge_ref, dmax_ref, qk_ref,
                 o_ref, t):
    ltile, csame, pos, mm = _chunk_consts(t)
    gates = slab_ref[0, :, 4 * MIX_W:] + bias_ref[...]
    li_e = _dot_xsel(gates, _head_expand(0))
    lf_e = _dot_xsel(-_softplus(-gates), _head_expand(HEADS))
    yield
    b_e = _dot_sel(ltile, lf_e)
    dlog = _dot_sel(ltile, jnp.where(pos > mm, lf_e, 0.0)) + _dot_sel(csame, jnp.where(pos == mm, li_e, 0.0))
    dl = jnp.where(mm <= pos, dlog, NEG)
    yield
    lane_head = _iota((t, MIX_W), 1) >> CHUNK_SHIFT
    dmax = jnp.full((t, MIX_W), NEG, F32)
    for h in range(HEADS):
        in_head = lane_head == h
        dmax = jnp.where(in_head, jnp.max(jnp.where(in_head, dl, NEG), axis=-1, keepdims=True), dmax)
    dl_ref[...] = dl
    dmax_ref[...] = dmax
    b_ref[...] = b_e
    ge_ref[...] = _chunk_last(b_e) - b_e + li_e
    yield

    ones_bd = _head_ones().astype(MXU_DTYPE)
    ones_rows = jnp.ones((CHUNK, MIX_W), MXU_DTYPE)
    chunk_rows = _chunk_slices(t)
    k_scale = HEAD_W ** -0.5
    for rows in chunk_rows:
        k_c = (slab_ref[0, rows, MIX_W:2 * MIX_W] * k_scale).astype(MXU_DTYPE)
        qk_ref[rows, :] = lax.dot_general(slab_ref[0, rows, 0:MIX_W].astype(MXU_DTYPE), _bd(k_c),
                                          (((1,), (1,)), ((), ())), preferred_element_type=F32)
    yield

    st = st_ref[...]
    m_st = m_ref[...]
    for rows in chunk_rows:
        q_c = slab_ref[0, rows, 0:MIX_W].astype(MXU_DTYPE)
        k_c = slab_ref[0, rows, MIX_W:2 * MIX_W] * k_scale
        v_c = slab_ref[0, rows, 2 * MIX_W:3 * MIX_W].astype(MXU_DTYPE)
        b_c = b_ref[rows, :]
        m_t = jnp.maximum(b_c + m_st, dmax_ref[rows, :])
        w_st = jnp.exp(b_c + m_st - m_t)
        s = qk_ref[rows, :] * jnp.exp(dl_ref[rows, :] - m_t)
        qs = jnp.dot(q_c, st.astype(MXU_DTYPE), preferred_element_type=F32)
        sv = jnp.dot(s.astype(MXU_DTYPE), jnp.concatenate([_bd(v_c), ones_bd], axis=1),
                     preferred_element_type=F32)
        num = w_st * qs[:, :MIX_W] + sv[:, :MIX_W]
        den = w_st * qs[:, MIX_W:] + sv[:, MIX_W:]
        o_ref[rows, :] = num / jnp.maximum(jnp.abs(den), jnp.exp(-m_t))
        ge_c = ge_ref[rows, :]
        b_last = b_c[CHUNK - 1:CHUNK, :]
        m_new = jnp.maximum(b_last + m_st, jnp.max(ge_c, axis=0, keepdims=True))
        w_old = jnp.exp(b_last + m_st - m_new)
        k_w = (k_c * jnp.exp(ge_c - m_new)).astype(MXU_DTYPE)
        upd = lax.dot_general(k_w, jnp.concatenate([v_c, ones_rows], axis=1), (((0,), (0,)), ((), ())),
                              preferred_element_type=F32)
        st = (st * jnp.concatenate([w_old, w_old], axis=1)
              + jnp.concatenate([_bd_keep(upd[:, :MIX_W]), _bd_keep(upd[:, MIX_W:])], axis=1))
        m_st = m_new
        yield
    st_ref[...] = st
    m_ref[...] = m_st

    hcat = o_ref[...] * _sigmoid(slab_ref[0, :, 3 * MIX_W:4 * MIX_W])
    ms = _dot_xsel(hcat * hcat, _head_ones(), terms=2) * (1.0 / HEAD_W)
    y_ref[0] = hcat * lax.rsqrt(ms + EPS) * g_ref[...]


_N_GDN_REFS = (5, 1, 16)
_N_SSD_REFS = (7, 1, 11)
_N_MLSTM_REFS = (3, 1, 8)


def _mixers_kernel(*refs, t):
    counts = (_N_GDN_REFS, _N_SSD_REFS, _N_MLSTM_REFS)
    groups = []
    pos = 0
    for kind in range(3):
        per_mixer = []
        for c in counts:
            per_mixer.append(refs[pos:pos + c[kind]])
            pos += c[kind]
        groups.append(per_mixer)
    gdn_refs, ssd_refs, ml_refs = [groups[0][i] + groups[1][i] + groups[2][i] for i in range(3)]

    @pl.when(pl.program_id(1) == 0)
    def _():
        for state_ref in (gdn_refs[6], gdn_refs[8], ssd_refs[8], ssd_refs[10], ml_refs[4], ml_refs[5]):
            state_ref[...] = jnp.zeros_like(state_ref)

    gdn = _gdn_steps(*gdn_refs, t)
    streams = [gdn, gdn, _ssd_steps(*ssd_refs, t), _mlstm_steps(*ml_refs, t)]
    while streams:
        for stream in list(streams):
            if stream in streams and next(stream, StopIteration) is StopIteration:
                streams = [s for s in streams if s is not stream]


def _mixers_call(gdn_slab, ssd_slab, ml_slab, p, bsz, seq, t=MIX_T):
    ch = 3 * MIX_W
    lane_pad = lambda v, before: jnp.pad(v, (before, LANES - before - v.shape[0])).reshape(1, LANES)
    full = lambda b, i: (0, 0)
    slab_spec = pl.BlockSpec((1, t, SLAB_W), lambda b, i: (b, i, 0))
    vec = lambda n: pl.BlockSpec((1, n), full)
    tile = lambda dtype=F32: pltpu.VMEM((t, MIX_W), dtype)
    out_spec = pl.BlockSpec((1, t, MIX_W), lambda b, i: (b, i, 0))
    out_shape = jax.ShapeDtypeStruct((bsz, seq, MIX_W), F32)
    conv_scratch = [pltpu.VMEM((8, ch), F32), pltpu.VMEM((t + 8, ch), F32)]
    gdn_in = [gdn_slab.reshape(bsz, seq, SLAB_W), p["gdn_conv_w"], lane_pad(p["gdn_a_log"], HEADS),
              lane_pad(p["gdn_dt_bias"], HEADS), jnp.tile(p["gdn_norm_g"], HEADS).reshape(1, MIX_W)]
    gdn_specs = [slab_spec, pl.BlockSpec((CONV_K, ch), full), vec(LANES), vec(LANES), vec(MIX_W)]
    gdn_scratch = conv_scratch + [pltpu.VMEM((MIX_W, MIX_W), F32)] + [tile() for _ in range(13)]
    ssd_in = [ssd_slab.reshape(bsz, seq, SLAB_W), p["ssd_conv_w"], p["ssd_conv_b"].reshape(1, ch),
              lane_pad(p["ssd_a_log"], 0), lane_pad(p["ssd_dt_bias"], 0),
              jnp.repeat(p["ssd_d"], HEAD_W).reshape(1, MIX_W), p["ssd_norm_g"].reshape(1, MIX_W)]
    ssd_specs = [slab_spec, pl.BlockSpec((CONV_K, ch), full), vec(ch), vec(LANES), vec(LANES), vec(MIX_W), vec(MIX_W)]
    ssd_scratch = (conv_scratch + [pltpu.VMEM((2, SSD_N, SSD_N), F32)] + [tile(MXU_DTYPE) for _ in range(4)]
                   + [tile() for _ in range(4)])
    ml_in = [ml_slab.reshape(bsz, seq, SLAB_W),
             lane_pad(jnp.concatenate([p["mlstm_i_bias"], p["mlstm_f_bias"]]), 0), p["mlstm_norm_g"].reshape(1, MIX_W)]
    ml_specs = [slab_spec, vec(LANES), vec(MIX_W)]
    ml_scratch = [pltpu.VMEM((MIX_W, 2 * MIX_W), F32), pltpu.VMEM((1, MIX_W), F32)] + [tile() for _ in range(6)]
    assert (len(gdn_in), 1, len(gdn_scratch)) == _N_GDN_REFS
    assert (len(ssd_in), 1, len(ssd_scratch)) == _N_SSD_REFS
    assert (len(ml_in), 1, len(ml_scratch)) == _N_MLSTM_REFS
    outs = pl.pallas_call(
        functools.partial(_mixers_kernel, t=t),
        out_shape=(out_shape, out_shape, out_shape),
        grid=(bsz, seq // t),
        in_specs=gdn_specs + ssd_specs + ml_specs,
        out_specs=(out_spec, out_spec, out_spec),
        scratch_shapes=gdn_scratch + ssd_scratch + ml_scratch,
        compiler_params=pltpu.CompilerParams(dimension_semantics=("parallel", "arbitrary"),
                                             vmem_limit_bytes=VMEM_LIMIT),
        name="recurrent_mixers",
    )(*gdn_in, *ssd_in, *ml_in)
    return [o.reshape(bsz * seq, MIX_W) for o in outs]


def _rope_tables(seq):
    inv_freq = ROPE_THETA ** (-jnp.arange(0, DIFF_DH, 2, dtype=F32) / DIFF_DH)
    ang = jnp.arange(seq, dtype=F32)[:, None] * inv_freq[None, :]
    cos, sin = jnp.cos(ang), jnp.sin(ang)
    reps = LANES // DIFF_DH
    cos_t = jnp.tile(jnp.concatenate([cos, cos], axis=-1), (1, reps))
    sin_t = jnp.tile(jnp.concatenate([-sin, sin], axis=-1), (1, reps))
    return cos_t, sin_t


def kernel(x, ffn1_pre_g, ffn1_w_gate, ffn1_w_up, ffn1_w_down, ffn1_post_g, mix_pre_g, w_in, gdn_conv_w, gdn_a_log, gdn_dt_bias, gdn_norm_g, diff_lam_q1, diff_lam_k1, diff_lam_q2, diff_lam_k2, diff_norm_g, ssd_conv_w, ssd_conv_b, ssd_a_log, ssd_dt_bias, ssd_d, ssd_norm_g, mlstm_i_bias, mlstm_f_bias, mlstm_norm_g, w_out, mix_post_g, ffn2_pre_g, ffn2_w_gate, ffn2_w_up, ffn2_w_down, ffn2_post_g):
    bsz, seq, d = x.shape
    x2 = x.reshape(bsz * seq, d)
    cos_t, sin_t = _rope_tables(seq)
    for l in range(DEPTH):
        lambda_init = 0.8 - 0.6 * math.exp(-0.3 * l)
        x2 = _ffn_call(x2, ffn1_pre_g[l], ffn1_w_gate[l], ffn1_w_up[l], ffn1_w_down[l], ffn1_post_g[l])
        gdn_slab, ssd_slab, ml_slab, dq, dk, dv = _proj_call(x2, mix_pre_g[l], w_in[l], cos_t, sin_t, seq)
        mixer_params = dict(
            gdn_conv_w=gdn_conv_w[l], gdn_a_log=gdn_a_log[l], gdn_dt_bias=gdn_dt_bias[l], gdn_norm_g=gdn_norm_g[l],
            ssd_conv_w=ssd_conv_w[l], ssd_conv_b=ssd_conv_b[l], ssd_a_log=ssd_a_log[l], ssd_dt_bias=ssd_dt_bias[l],
            ssd_d=ssd_d[l], ssd_norm_g=ssd_norm_g[l],
            mlstm_i_bias=mlstm_i_bias[l], mlstm_f_bias=mlstm_f_bias[l], mlstm_norm_g=mlstm_norm_g[l])
        out_a, out_c, out_d = _mixers_call(gdn_slab, ssd_slab, ml_slab, mixer_params, bsz, seq)
        lam_p = jnp.stack([diff_lam_q1[l], diff_lam_k1[l], diff_lam_q2[l], diff_lam_k2[l]])
        out_b = _attn_call(dq, dk, dv, lam_p, diff_norm_g[l], lambda_init, bsz, seq)
        x2 = _ffn_call(x2, ffn2_pre_g[l], ffn2_w_gate[l], ffn2_w_up[l], ffn2_w_down[l], ffn2_post_g[l],
                       mix=((out_a, out_b, out_c, out_d), w_out[l], mix_post_g[l]))
    return x2.reshape(bsz, seq, d)
```

```python
import functools
import math

import jax
import jax.numpy as jnp
from jax import lax
from jax.experimental import pallas as pl
from jax.experimental.pallas import tpu as pltpu

F32 = jnp.float32
MXU_DTYPE = jnp.bfloat16

D_MODEL = 1024
D_FF = 2816
DEPTH = 2
CHUNK = 64
CHUNK_SHIFT = 6
CONV_K = 4
EPS = 1e-6
ROPE_THETA = 10000.0
HEADS = 4
HEAD_W = 64
MIX_W = HEADS * HEAD_W
DIFF_DH = 32
DIFF_DH_SHIFT = 5
LOG2_E = math.log2(math.e)
SSD_N = 128
LANES = 128
NEG = -1e30

_GDN_COLS = (0, 1032)
_DIFF_COLS = (1032, 1800)
_SSD_COLS = (1800, 2828)
_MLSTM_COLS = (2828, 3860)
SLAB_W = 1152

FFN_TM = 512
FFN_FC = 256
MIX_T = 256
MIX_NB = 2
ATT_TQ = 256
ATT_RB = 16
ATT_TK = FFN_TM
VMEM_LIMIT = 56 * 1024 * 1024


def _dot(a, b):
    return jnp.dot(a.astype(MXU_DTYPE), b.astype(MXU_DTYPE), preferred_element_type=F32)


def _dot_nt(a, b):
    return lax.dot_general(a.astype(MXU_DTYPE), b.astype(MXU_DTYPE), (((1,), (1,)), ((), ())),
                           preferred_element_type=F32)


def _dot_tn(a, b):
    return lax.dot_general(a.astype(MXU_DTYPE), b.astype(MXU_DTYPE), (((0,), (0,)), ((), ())),
                           preferred_element_type=F32)


def _split3(x):
    x1 = x.astype(MXU_DTYPE)
    r1 = x - x1.astype(F32)
    x2 = r1.astype(MXU_DTYPE)
    return x1, x2, (r1 - x2.astype(F32)).astype(MXU_DTYPE)


def _dot_sel(sel, x, terms=3):
    sel = sel.astype(MXU_DTYPE)
    return sum(jnp.dot(sel, xi, preferred_element_type=F32) for xi in _split3(x)[:terms])


def _dot_xsel(x, sel, terms=3):
    sel = sel.astype(MXU_DTYPE)
    return sum(jnp.dot(xi, sel, preferred_element_type=F32) for xi in _split3(x)[:terms])


def _chunk_last(x):
    t, w = x.shape
    return jnp.concatenate([jnp.broadcast_to(x[c + CHUNK - 1:c + CHUNK, :], (CHUNK, w))
                            for c in range(0, t, CHUNK)], axis=0)


def _rms(x, g):
    return x * lax.rsqrt(jnp.mean(x * x, axis=-1, keepdims=True) + EPS) * g


def _silu(x):
    return x / (1.0 + jnp.exp(-x))


def _sigmoid(x):
    return 1.0 / (1.0 + jnp.exp(-x))


def _softplus(x):
    return jnp.maximum(x, 0.0) + jnp.log(1.0 + jnp.exp(-jnp.abs(x)))


def _iota(shape, dim):
    return lax.broadcasted_iota(jnp.int32, shape, dim)


def _chunk_consts(t):
    r = _iota((t, t), 0)
    c = _iota((t, t), 1)
    same = (r >> CHUNK_SHIFT) == (c >> CHUNK_SHIFT)
    ltile = jnp.where(same & (c <= r), 1.0, 0.0).astype(F32)
    csame = jnp.where(same, 1.0, 0.0).astype(F32)
    pos = _iota((t, MIX_W), 0) & (CHUNK - 1)
    m = _iota((t, MIX_W), 1) & (CHUNK - 1)
    return ltile, csame, pos, m


def _head_expand(first_lane):
    r = _iota((LANES, MIX_W), 0)
    c = _iota((LANES, MIX_W), 1)
    return jnp.where(r == (c >> CHUNK_SHIFT) + first_lane, 1.0, 0.0).astype(F32)


def _head_mask():
    return (_iota((MIX_W, MIX_W), 0) >> CHUNK_SHIFT) == (_iota((MIX_W, MIX_W), 1) >> CHUNK_SHIFT)


def _head_ones():
    return jnp.where(_head_mask(), 1.0, 0.0).astype(F32)


def _causal_conv(u, w_ref, carry_ref, buf_ref, t):
    buf_ref[0:8, :] = carry_ref[...]
    buf_ref[8:8 + t, :] = u
    carry_ref[...] = u[t - 8:t, :]
    acc = w_ref[CONV_K - 1:CONV_K, :] * u
    for j in range(CONV_K - 1):
        off = 8 - (CONV_K - 1) + j
        acc = acc + w_ref[j:j + 1, :] * buf_ref[off:off + t, :]
    return acc


def _ffn_body(x, pre_ref, wg_ref, wu_ref, wd_ref, post_ref, h_ref, acc_ref, fc):
    h_ref[...] = _rms(x, pre_ref[...]).astype(h_ref.dtype)
    acc_ref[...] = jnp.zeros_like(acc_ref)
    for c0 in range(0, wg_ref.shape[1], fc):
        g = jnp.dot(h_ref[...], wg_ref[:, c0:c0 + fc], preferred_element_type=F32)
        u = jnp.dot(h_ref[...], wu_ref[:, c0:c0 + fc], preferred_element_type=F32)
        a = (_silu(g) * u).astype(wd_ref.dtype)
        acc_ref[...] += jnp.dot(a, wd_ref[c0:c0 + fc, :], preferred_element_type=F32)
    return x + 0.5 * _rms(acc_ref[...], post_ref[...])


def _ffn_kernel(x_ref, pre_ref, wg_ref, wu_ref, wd_ref, post_ref, o_ref, h_ref, acc_ref, *, fc):
    o_ref[...] = _ffn_body(x_ref[...], pre_ref, wg_ref, wu_ref, wd_ref, post_ref, h_ref, acc_ref, fc)


def _outproj_ffn_kernel(x_ref, a_ref, b_ref, c_ref, d_ref, wo_ref, go_ref, pre_ref, wg_ref, wu_ref, wd_ref, post_ref,
                        o_ref, h_ref, acc_ref, *, fc):
    mixed = jnp.concatenate([a_ref[...], b_ref[...], c_ref[...], d_ref[...]], axis=-1).astype(wo_ref.dtype)
    x = x_ref[...] + _rms(jnp.dot(mixed, wo_ref[...], preferred_element_type=F32), go_ref[...])
    o_ref[...] = _ffn_body(x, pre_ref, wg_ref, wu_ref, wd_ref, post_ref, h_ref, acc_ref, fc)


def _resident(shape):
    return pl.BlockSpec(shape, lambda i: (0,) * len(shape), pipeline_mode=pl.Buffered(1))


def _ffn_call(x2, pre_g, w_gate, w_up, w_down, post_g, mix=None, tm=FFN_TM, fc=FFN_FC):
    m, d = x2.shape
    ff = w_gate.shape[1]
    row = lambda i: (i, 0)
    ffn_specs = [_resident((1, d)), _resident((d, ff)), _resident((d, ff)), _resident((ff, d)), _resident((1, d))]
    ffn_args = (pre_g.reshape(1, d), w_gate.astype(MXU_DTYPE), w_up.astype(MXU_DTYPE), w_down.astype(MXU_DTYPE),
                post_g.reshape(1, d))
    if mix is None:
        body, name = _ffn_kernel, "ffn"
        specs, args = ffn_specs, ffn_args
    else:
        outs, w_out, mix_post_g = mix
        body, name = _outproj_ffn_kernel, "mixer_out_proj_ffn"
        specs = [pl.BlockSpec((tm, MIX_W), row)] * 4 + [_resident((d, d)), _resident((1, d))] + ffn_specs
        args = (*outs, w_out.astype(MXU_DTYPE), mix_post_g.reshape(1, d)) + ffn_args
    return pl.pallas_call(
        functools.partial(body, fc=fc),
        out_shape=jax.ShapeDtypeStruct((m, d), F32),
        grid=(m // tm,),
        in_specs=[pl.BlockSpec((tm, d), row)] + specs,
        out_specs=pl.BlockSpec((tm, d), row),
        scratch_shapes=[pltpu.VMEM((tm, d), MXU_DTYPE), pltpu.VMEM((tm, d), F32)],
        compiler_params=pltpu.CompilerParams(dimension_semantics=("parallel",), vmem_limit_bytes=VMEM_LIMIT),
        name=name,
    )(x2, *args)


def _rope_half(x, cos, sin):
    lane = _iota(x.shape, 1)
    half = DIFF_DH // 2
    partner = jnp.where((lane & (DIFF_DH - 1)) < half, pltpu.roll(x, LANES - half, 1), pltpu.roll(x, half, 1))
    return x * cos + partner * sin


def _proj_kernel(x_ref, g_ref, wa_ref, wb_ref, wc_ref, wd_ref, cos_ref, sin_ref,
                 gdn_ref, ssd_ref, ml_ref, dq_ref, dk_ref, dv_ref):
    h = _rms(x_ref[...], g_ref[...]).astype(wa_ref.dtype)
    gdn_ref[...] = jnp.dot(h, wa_ref[...], preferred_element_type=F32)
    ssd_ref[...] = jnp.dot(h, wc_ref[...], preferred_element_type=F32)
    ml_ref[...] = jnp.dot(h, wd_ref[...], preferred_element_type=F32)
    d = jnp.dot(h, wb_ref[...], preferred_element_type=F32)
    cos = cos_ref[...]
    sin = sin_ref[...]
    scale = DIFF_DH ** -0.5 * LOG2_E
    for half in range(2):
        lo = half * LANES
        dq_ref[:, lo:lo + LANES] = (_rope_half(d[:, lo:lo + LANES], cos, sin) * scale).astype(dq_ref.dtype)
        dk_ref[:, lo:lo + LANES] = _rope_half(d[:, MIX_W + lo:MIX_W + lo + LANES], cos, sin).astype(dk_ref.dtype)
    ones = jnp.ones((d.shape[0], HEAD_W), F32)
    for h in range(HEADS):
        v_h = d[:, 2 * MIX_W + h * HEAD_W:2 * MIX_W + (h + 1) * HEAD_W]
        dv_ref[0, h * LANES:(h + 1) * LANES, :] = jnp.concatenate([v_h, ones], axis=-1).T.astype(dv_ref.dtype)


def _pad_cols(w, width):
    return jnp.pad(w, ((0, 0), (0, width - w.shape[1])))


def _proj_call(x2, pre_g, w_in, cos_t, sin_t, seq, tm=FFN_TM):
    m, d = x2.shape
    wa = _pad_cols(w_in[:, _GDN_COLS[0]:_GDN_COLS[1]], SLAB_W).astype(MXU_DTYPE)
    wb = w_in[:, _DIFF_COLS[0]:_DIFF_COLS[1]].astype(MXU_DTYPE)
    wc = _pad_cols(w_in[:, _SSD_COLS[0]:_SSD_COLS[1]], SLAB_W).astype(MXU_DTYPE)
    wd = _pad_cols(w_in[:, _MLSTM_COLS[0]:_MLSTM_COLS[1]], SLAB_W).astype(MXU_DTYPE)
    tiles_per_seq = seq // tm
    full = lambda i: (0, 0)
    row = lambda i: (i, 0)
    return pl.pallas_call(
        _proj_kernel,
        out_shape=(
            jax.ShapeDtypeStruct((m, SLAB_W), F32),
            jax.ShapeDtypeStruct((m, SLAB_W), F32),
            jax.ShapeDtypeStruct((m, SLAB_W), F32),
            jax.ShapeDtypeStruct((m, MIX_W), MXU_DTYPE),
            jax.ShapeDtypeStruct((m, MIX_W), MXU_DTYPE),
            jax.ShapeDtypeStruct((m // tm, HEADS * LANES, tm), MXU_DTYPE),
        ),
        grid=(m // tm,),
        in_specs=[
            pl.BlockSpec((tm, d), row),
            pl.BlockSpec((1, d), full),
            pl.BlockSpec((d, SLAB_W), full),
            pl.BlockSpec((d, 3 * MIX_W), full),
            pl.BlockSpec((d, SLAB_W), full),
            pl.BlockSpec((d, SLAB_W), full),
            pl.BlockSpec((tm, LANES), lambda i: (i % tiles_per_seq, 0)),
            pl.BlockSpec((tm, LANES), lambda i: (i % tiles_per_seq, 0)),
        ],
        out_specs=(
            pl.BlockSpec((tm, SLAB_W), row),
            pl.BlockSpec((tm, SLAB_W), row),
            pl.BlockSpec((tm, SLAB_W), row),
            pl.BlockSpec((tm, MIX_W), row),
            pl.BlockSpec((tm, MIX_W), row),
            pl.BlockSpec((1, HEADS * LANES, tm), lambda i: (i, 0, 0)),
        ),
        compiler_params=pltpu.CompilerParams(dimension_semantics=("parallel",), vmem_limit_bytes=VMEM_LIMIT),
        name="mixer_in_proj",
    )(x2, pre_g.reshape(1, d), wa, wb, wc, wd, cos_t, sin_t)


def _attn_kernel(q_ref, k_ref, vt_ref, lam_ref, g_ref, o_ref, qt_ref, m_ref, acc_ref, p_ref, s_ref, alpha_ref,
                 *, tq, tk, lambda_init):
    i = pl.program_id(1)
    @pl.when(i == 0)
    def _():
        qt_ref[...] = jnp.zeros_like(qt_ref)

    q_t = q_ref[0].astype(F32).T
    for mp in range(2 * HEADS):
        dims = slice(mp * DIFF_DH, (mp + 1) * DIFF_DH)
        qt_ref[mp // 2, dims, (mp % 2) * tq:(mp % 2 + 1) * tq] = q_t[dims, :].astype(qt_ref.dtype)
    m_ref[...] = jnp.full_like(m_ref, NEG)
    acc_ref[...] = jnp.zeros_like(acc_ref)
    qpos = i * tq + (_iota((1, 2 * tq), 1) & (tq - 1))
    limit = ((qpos >> CHUNK_SHIFT) + 1) << CHUNK_SHIFT
    n_kt = (i * tq + tq + tk - 1) // tk

    def scores(h, j, rows=tk):
        kt = k_ref[0, pl.ds(pl.multiple_of(j * tk, tk), rows), :]
        s_ref[h, 0:rows, :] = jnp.dot(kt, qt_ref[h], preferred_element_type=F32)

    def softmax(h, j, rows, mask_from):
        def piece(r0):
            sp = s_ref[h, r0:r0 + ATT_RB, :]
            if mask_from is not None and r0 >= mask_from:
                key = j * tk + r0 + _iota((ATT_RB, 1), 0)
                sp = jnp.where(key < limit, sp, NEG)
            return sp

        mx = piece(0)
        for r0 in range(ATT_RB, rows, ATT_RB):
            mx = jnp.maximum(mx, piece(r0))
        m_old = m_ref[h]
        m_new = jnp.maximum(m_old, jnp.max(mx, axis=0, keepdims=True))
        for r0 in range(0, rows, ATT_RB):
            p_ref[h, r0:r0 + ATT_RB, :] = jnp.exp2(piece(r0) - m_new).astype(p_ref.dtype)
        alpha_ref[h] = jnp.exp2(m_old - m_new)
        m_ref[h] = m_new

    def values(h, j, rows):
        pv = jnp.dot(vt_ref[j, h * LANES:(h + 1) * LANES, 0:rows], p_ref[h, 0:rows, :],
                     preferred_element_type=F32)
        acc_ref[h] = alpha_ref[h] * acc_ref[h] + pv

    def step(j, rows, mask_from, prefetch_scores):
        for h in range(HEADS):
            softmax(h, j, rows, mask_from)
            if h + 2 < HEADS:
                scores(h + 2, j, rows)
            elif prefetch_scores:
                scores(h + 2 - HEADS, j + 1)
            values(h, j, rows)

    def body(j, carry):
        step(j, tk, None, True)
        return carry

    scores(0, 0)
    scores(1, 0)
    lax.fori_loop(0, n_kt - 1, body, 0)
    last = n_kt - 1
    diagonal_first = (i * tq) == last * tk

    @pl.when(diagonal_first)
    def _():
        step(last, tq, 0, False)

    @pl.when(jnp.logical_not(diagonal_first))
    def _():
        step(last, tk, tq, False)

    lam_p = lam_ref[...]
    lam = (jnp.exp(jnp.sum(lam_p[0:1] * lam_p[1:2], axis=-1, keepdims=True))
           - jnp.exp(jnp.sum(lam_p[2:3] * lam_p[3:4], axis=-1, keepdims=True)) + lambda_init)
    outs = []
    for h in range(HEADS):
        a = acc_ref[h]
        o = (a[:HEAD_W, :tq] / a[HEAD_W:HEAD_W + 1, :tq]
             - lam * (a[:HEAD_W, tq:] / a[HEAD_W:HEAD_W + 1, tq:]))
        outs.append(o * lax.rsqrt(jnp.mean(o * o, axis=0, keepdims=True) + EPS))
    o_ref[0] = jnp.concatenate(outs, axis=0).T * g_ref[...] * (1.0 - lambda_init)


def _attn_call(dq, dk, dvt, lam_p, norm_g, lambda_init, bsz, seq, tq=ATT_TQ, tk=ATT_TK):
    assert tk == 2 * tq, "the last-key-tile split in _attn_kernel assumes tk == 2 tq"
    q3 = dq.reshape(bsz, seq, MIX_W)
    k3 = dk.reshape(bsz, seq, MIX_W)
    n_kt = seq // tk
    out = pl.pallas_call(
        functools.partial(_attn_kernel, tq=tq, tk=tk, lambda_init=lambda_init),
        out_shape=jax.ShapeDtypeStruct((bsz, seq, MIX_W), F32),
        grid=(bsz, seq // tq),
        in_specs=[
            pl.BlockSpec((1, tq, MIX_W), lambda b, i: (b, i, 0)),
            pl.BlockSpec((1, seq, MIX_W), lambda b, i: (b, 0, 0)),
            pl.BlockSpec((n_kt, HEADS * LANES, tk), lambda b, i: (b, 0, 0)),
            pl.BlockSpec((4, DIFF_DH), lambda b, i: (0, 0)),
            pl.BlockSpec((1, MIX_W), lambda b, i: (0, 0)),
        ],
        out_specs=pl.BlockSpec((1, tq, MIX_W), lambda b, i: (b, i, 0)),
        scratch_shapes=[
            pltpu.VMEM((HEADS, MIX_W, 2 * tq), MXU_DTYPE),
            pltpu.VMEM((HEADS, 1, 2 * tq), F32),
            pltpu.VMEM((HEADS, LANES, 2 * tq), F32),
            pltpu.VMEM((HEADS, tk, 2 * tq), MXU_DTYPE),
            pltpu.VMEM((HEADS, tk, 2 * tq), F32),
            pltpu.VMEM((HEADS, 1, 2 * tq), F32),
        ],
        compiler_params=pltpu.CompilerParams(dimension_semantics=("parallel", "arbitrary"),
                                             vmem_limit_bytes=VMEM_LIMIT),
        name="diff_attention",
    )(q3, k3, dvt, lam_p, jnp.tile(norm_g, HEADS).reshape(1, MIX_W))
    return out.reshape(bsz * seq, MIX_W)


def _head_block(x, h):
    blk = x[:, (h // 2) * LANES:(h // 2 + 1) * LANES]
    own = (_iota(blk.shape, 1) >> CHUNK_SHIFT) == (h % 2)
    return jnp.where(own, blk, jnp.zeros_like(blk))


def _bd_rows(blocks):
    zero = jnp.zeros_like(blocks[0])
    return jnp.concatenate([jnp.concatenate([b, zero] if h < 2 else [zero, b], axis=1)
                            for h, b in enumerate(blocks)], axis=0)


def _bd(x):
    return _bd_rows([_head_block(x, h) for h in range(HEADS)])


def _bd_keep(m):
    return _bd_rows([_head_block(m[h * CHUNK:(h + 1) * CHUNK, :], h) for h in range(HEADS)])


def _split(x):
    hi = x.astype(MXU_DTYPE)
    return hi, (x - hi.astype(F32)).astype(MXU_DTYPE)


def _bd_matmul3(a, b):
    ah, al = _split(a)
    bh, bl = _split(b)
    m = a.shape[0]
    r = jnp.dot(jnp.concatenate([ah, al], axis=0), _bd(bh), preferred_element_type=F32)
    return r[:m] + r[m:] + jnp.dot(ah, _bd(bl), preferred_element_type=F32)


def _chunk_slices(t):
    return [slice(c * CHUNK, (c + 1) * CHUNK) for c in range(t // CHUNK)]


def _ssd_steps(slab_ref, cw_ref, cb_ref, alog_ref, dtb_ref, dskip_ref, g_ref, y_ref,
               carry_ref, buf_ref, st_ref, xc_ref, xcd_ref, bm_ref, cm_ref, lm_ref, ea_ref, el_ref, yo_ref, t, b):
    ltile, _, pos, mm = _chunk_consts(t)
    expand = _head_expand(0)
    xbc = _silu(_causal_conv(slab_ref[b,:, MIX_W:4 * MIX_W], cw_ref, carry_ref, buf_ref, t) + cb_ref[...])
    x = xbc[:, 0:MIX_W]
    bm_ref[...] = xbc[:, MIX_W:2 * MIX_W].astype(bm_ref.dtype)
    cm_ref[...] = xbc[:, 2 * MIX_W:3 * MIX_W].astype(cm_ref.dtype)
    yield
    dt = _softplus(slab_ref[b,:, 4 * MIX_W:] + dtb_ref[...])
    da = dt * (-jnp.exp(alog_ref[...]))
    dt_e = _dot_xsel(dt, expand)
    da_e = _dot_xsel(da, expand)
    yield
    acs_e = _dot_sel(ltile, da_e)
    acs_last_e = _chunk_last(acs_e)
    ldiff = _dot_sel(ltile, jnp.where(pos > mm, da_e, 0.0))
    lm_ref[...] = jnp.where(mm <= pos, jnp.exp(ldiff), 0.0)
    xc = x * dt_e
    xc_ref[...] = xc.astype(xc_ref.dtype)
    xcd_ref[...] = (xc * jnp.exp(acs_last_e - acs_e)).astype(xcd_ref.dtype)
    ea_ref[...] = jnp.exp(acs_e)
    el_ref[...] = jnp.exp(acs_last_e)
    yield
    for c in range(t // CHUNK):
        r0 = c * CHUNK
        rows = slice(r0, r0 + CHUNK)
        for grp in range(2):
            gl = slice(grp * SSD_N, (grp + 1) * SSD_N)
            bg = bm_ref[rows, gl]
            cg = cm_ref[rows, gl]
            cb = _dot_nt(cg, bg)
            ys = []
            for hh in range(2):
                hl = slice(grp * SSD_N + hh * HEAD_W, grp * SSD_N + (hh + 1) * HEAD_W)
                ys.append(_dot(cb * lm_ref[rows, hl], xc_ref[rows, hl]))
            y_diag = jnp.concatenate(ys, axis=-1)
            st = st_ref[grp]
            y_off = _dot(cg, st) * ea_ref[rows, gl]
            st_ref[grp] = st * el_ref[r0:r0 + 1, gl] + _dot_tn(bg, xcd_ref[rows, gl])
            yo_ref[rows, gl] = y_diag + y_off
        yield
    y = (yo_ref[...] + x * dskip_ref[...]) * _silu(slab_ref[b,:, 0:MIX_W])
    gs = g_ref[...]
    for grp in range(2):
        gl = slice(grp * SSD_N, (grp + 1) * SSD_N)
        y_ref[b, :, gl] = _rms(y[:, gl], gs[:, gl])


def _gdn_steps(slab_ref, cw_ref, alog_ref, dtb_ref, g_ref, y_ref,
               carry_ref, buf_ref, st_ref, qn_ref, kn_ref, qd_ref, kd_ref, kb_ref, vb_ref, kbg_ref, dec_ref,
               egl_ref, u_ref, w_ref, attn_ref, o_ref, t, b):
    ltile, _, pos, mm = _chunk_consts(t)
    ones_h = _head_ones()
    qkv = _silu(_causal_conv(slab_ref[b,:, 0:3 * MIX_W], cw_ref, carry_ref, buf_ref, t))
    q = qkv[:, 0:MIX_W]
    k = qkv[:, MIX_W:2 * MIX_W]
    v = qkv[:, 2 * MIX_W:]
    yield
    qn = q * lax.rsqrt(_dot_xsel(q * q, ones_h, terms=2) + EPS) * (HEAD_W ** -0.5)
    kn = k * lax.rsqrt(_dot_xsel(k * k, ones_h, terms=2) + EPS)
    yield
    ba = slab_ref[b,:, 4 * MIX_W:]
    beta_e = _dot_xsel(_sigmoid(ba), _head_expand(0))
    g4 = -jnp.exp(alog_ref[...]) * _softplus(ba + dtb_ref[...])
    g_e = _dot_xsel(g4, _head_expand(HEADS))
    yield
    gc_e = _dot_sel(ltile, g_e)
    gl_e = _chunk_last(gc_e)
    diff = _dot_sel(ltile, jnp.where(pos > mm, g_e, 0.0))
    dec_ref[...] = jnp.where(mm <= pos, jnp.exp(diff), 0.0)
    egc = jnp.exp(gc_e)
    kb = kn * beta_e
    qn_ref[...] = qn
    kn_ref[...] = kn
    qd_ref[...] = qn * egc
    kd_ref[...] = kn * jnp.exp(gl_e - gc_e)
    kb_ref[...] = kb
    vb_ref[...] = v * beta_e
    kbg_ref[...] = kb * egc
    egl_ref[...] = jnp.exp(gl_e)
    yield

    r_ss = _iota((CHUNK, MIX_W), 0)
    m_ss = _iota((CHUNK, MIX_W), 1) & (CHUNK - 1)
    eye_ss = jnp.where(r_ss == m_ss, 1.0, 0.0).astype(F32)
    chunk_rows = _chunk_slices(t)

    ys, tms = [], []
    for rows in chunk_rows:
        lhs = jnp.concatenate([kb_ref[rows, :], qn_ref[rows, :]], axis=0).astype(MXU_DTYPE)
        raw = lax.dot_general(lhs, _bd(kn_ref[rows, :].astype(MXU_DTYPE)), (((1,), (1,)), ((), ())),
                              preferred_element_type=F32)
        dec = dec_ref[rows, :]
        attn_ref[rows, :] = raw[CHUNK:] * dec
        y = jnp.where(m_ss < r_ss, -(raw[:CHUNK] * dec), 0.0)
        ys.append(y)
        tms.append(eye_ss + y)
    yield
    ys = [_bd_matmul3(y, y) for y in ys]
    yield
    for _ in range(4):
        rs = [_bd_matmul3(jnp.concatenate([tm, y], axis=0), y) for tm, y in zip(tms, ys)]
        tms = [tm + r[:CHUNK] for tm, r in zip(tms, rs)]
        ys = [r[CHUNK:] for r in rs]
        yield
    tms = [tm + _bd_matmul3(tm, y) for tm, y in zip(tms, ys)]
    yield
    for rows, tm in zip(chunk_rows, tms):
        rhs = jnp.concatenate([_bd(vb_ref[rows, :].astype(MXU_DTYPE)),
                               _bd(kbg_ref[rows, :].astype(MXU_DTYPE))], axis=1)
        uw = jnp.dot(tm.astype(MXU_DTYPE), rhs, preferred_element_type=F32)
        u_ref[rows, :] = uw[:, :MIX_W]
        w_ref[rows, :] = uw[:, MIX_W:]
    yield

    st = st_ref[...]
    for c, rows in enumerate(chunk_rows):
        lhs = jnp.concatenate([w_ref[rows, :], qd_ref[rows, :]], axis=0).astype(MXU_DTYPE)
        r = jnp.dot(lhs, st.astype(MXU_DTYPE), preferred_element_type=F32)
        v_new = u_ref[rows, :] - r[:CHUNK]
        v_new_c = v_new.astype(MXU_DTYPE)
        o_ref[rows, :] = r[CHUNK:] + jnp.dot(attn_ref[rows, :].astype(MXU_DTYPE), _bd(v_new_c),
                                             preferred_element_type=F32)
        kv = lax.dot_general(kd_ref[rows, :].astype(MXU_DTYPE), v_new_c, (((0,), (0,)), ((), ())),
                             preferred_element_type=F32)
        st = st * egl_ref[c * CHUNK:c * CHUNK + 1, :] + _bd_keep(kv)
        yield
    st_ref[...] = st

    o = o_ref[...]
    o = o * lax.rsqrt(_dot_xsel(o * o, ones_h, terms=2) * (1.0 / HEAD_W) + EPS) * g_ref[...]
    y_ref[b] =o * _silu(slab_ref[b,:, 3 * MIX_W:4 * MIX_W])


def _mlstm_steps(slab_ref, bias_ref, g_ref, y_ref, st_ref, m_ref, dl_ref, b_ref, ge_ref, dmax_ref, qk_ref,
                 o_ref, t, b):
    ltile, csame, pos, mm = _chunk_consts(t)
    gates = slab_ref[b,:, 4 * MIX_W:] + bias_ref[...]
    li_e = _dot_xsel(gates, _head_expand(0))
    lf_e = _dot_xsel(-_softplus(-gates), _head_expand(HEADS))
    yield
    b_e = _dot_sel(ltile, lf_e)
    dlog = _dot_sel(ltile, jnp.where(pos > mm, lf_e, 0.0)) + _dot_sel(csame, jnp.where(pos == mm, li_e, 0.0))
    dl = jnp.where(mm <= pos, dlog, NEG)
    yield
    lane_head = _iota((t, MIX_W), 1) >> CHUNK_SHIFT
    dmax = jnp.full((t, MIX_W), NEG, F32)
    for h in range(HEADS):
        in_head = lane_head == h
        dmax = jnp.where(in_head, jnp.max(jnp.where(in_head, dl, NEG), axis=-1, keepdims=True), dmax)
    dl_ref[...] = dl
    dmax_ref[...] = dmax
    b_ref[...] = b_e
    ge_ref[...] = _chunk_last(b_e) - b_e + li_e
    yield

    ones_bd = _head_ones().astype(MXU_DTYPE)
    ones_rows = jnp.ones((CHUNK, MIX_W), MXU_DTYPE)
    chunk_rows = _chunk_slices(t)
    k_scale = HEAD_W ** -0.5
    for rows in chunk_rows:
        k_c = (slab_ref[b,rows, MIX_W:2 * MIX_W] * k_scale).astype(MXU_DTYPE)
        qk_ref[rows, :] = lax.dot_general(slab_ref[b,rows, 0:MIX_W].astype(MXU_DTYPE), _bd(k_c),
                                          (((1,), (1,)), ((), ())), preferred_element_type=F32)
    yield

    st = st_ref[...]
    m_st = m_ref[...]
    for rows in chunk_rows:
        q_c = slab_ref[b,rows, 0:MIX_W].astype(MXU_DTYPE)
        k_c = slab_ref[b,rows, MIX_W:2 * MIX_W] * k_scale
        v_c = slab_ref[b,rows, 2 * MIX_W:3 * MIX_W].astype(MXU_DTYPE)
        b_c = b_ref[rows, :]
        m_t = jnp.maximum(b_c + m_st, dmax_ref[rows, :])
        w_st = jnp.exp(b_c + m_st - m_t)
        s = qk_ref[rows, :] * jnp.exp(dl_ref[rows, :] - m_t)
        qs = jnp.dot(q_c, st.astype(MXU_DTYPE), preferred_element_type=F32)
        sv = jnp.dot(s.astype(MXU_DTYPE), jnp.concatenate([_bd(v_c), ones_bd], axis=1),
                     preferred_element_type=F32)
        num = w_st * qs[:, :MIX_W] + sv[:, :MIX_W]
        den = w_st * qs[:, MIX_W:] + sv[:, MIX_W:]
        o_ref[rows, :] = num / jnp.maximum(jnp.abs(den), jnp.exp(-m_t))
        ge_c = ge_ref[rows, :]
        b_last = b_c[CHUNK - 1:CHUNK, :]
        m_new = jnp.maximum(b_last + m_st, jnp.max(ge_c, axis=0, keepdims=True))
        w_old = jnp.exp(b_last + m_st - m_new)
        k_w = (k_c * jnp.exp(ge_c - m_new)).astype(MXU_DTYPE)
        upd = lax.dot_general(k_w, jnp.concatenate([v_c, ones_rows], axis=1), (((0,), (0,)), ((), ())),
                              preferred_element_type=F32)
        st = (st * jnp.concatenate([w_old, w_old], axis=1)
              + jnp.concatenate([_bd_keep(upd[:, :MIX_W]), _bd_keep(upd[:, MIX_W:])], axis=1))
        m_st = m_new
        yield
    st_ref[...] = st
    m_ref[...] = m_st

    hcat = o_ref[...] * _sigmoid(slab_ref[b,:, 3 * MIX_W:4 * MIX_W])
    ms = _dot_xsel(hcat * hcat, _head_ones(), terms=2) * (1.0 / HEAD_W)
    y_ref[b] =hcat * lax.rsqrt(ms + EPS) * g_ref[...]


_N_GDN_REFS = (5, 1, 16)
_N_SSD_REFS = (7, 1, 11)
_N_MLSTM_REFS = (3, 1, 8)


def _mixers_kernel(*refs, t, nb):
    counts = (_N_GDN_REFS, _N_SSD_REFS, _N_MLSTM_REFS)
    pos = 0
    ins, outs = [], []
    for c in counts:
        ins.append(refs[pos:pos + c[0]])
        pos += c[0]
    for c in counts:
        outs.append(refs[pos:pos + c[1]])
        pos += c[1]
    scratch = []
    for b in range(nb):
        per_mixer = []
        for c in counts:
            per_mixer.append(refs[pos:pos + c[2]])
            pos += c[2]
        scratch.append(per_mixer)

    @pl.when(pl.program_id(1) == 0)
    def _():
        for b in range(nb):
            gdn_s, ssd_s, ml_s = scratch[b]
            for state_ref in (gdn_s[0], gdn_s[2], ssd_s[0], ssd_s[2], ml_s[0], ml_s[1]):
                state_ref[...] = jnp.zeros_like(state_ref)

    gdn = [_gdn_steps(*ins[0], *outs[0], *scratch[b][0], t, b) for b in range(nb)]
    ssd = [_ssd_steps(*ins[1], *outs[1], *scratch[b][1], t, b) for b in range(nb)]
    mls = [_mlstm_steps(*ins[2], *outs[2], *scratch[b][2], t, b) for b in range(nb)]
    streams = gdn + gdn + ssd + mls
    while streams:
        for stream in list(streams):
            if stream in streams and next(stream, StopIteration) is StopIteration:
                streams = [s for s in streams if s is not stream]


def _mixers_call(gdn_slab, ssd_slab, ml_slab, p, bsz, seq, t=MIX_T, nb=MIX_NB):
    ch = 3 * MIX_W
    lane_pad = lambda v, before: jnp.pad(v, (before, LANES - before - v.shape[0])).reshape(1, LANES)
    full = lambda b, i: (0, 0)
    slab_spec = pl.BlockSpec((nb, t, SLAB_W), lambda b, i: (b, i, 0))
    vec = lambda n: pl.BlockSpec((1, n), full)
    tile = lambda dtype=F32: pltpu.VMEM((t, MIX_W), dtype)
    out_spec = pl.BlockSpec((nb, t, MIX_W), lambda b, i: (b, i, 0))
    out_shape = jax.ShapeDtypeStruct((bsz, seq, MIX_W), F32)
    conv_scratch = lambda: [pltpu.VMEM((8, ch), F32), pltpu.VMEM((t + 8, ch), F32)]
    gdn_in = [gdn_slab.reshape(bsz, seq, SLAB_W), p["gdn_conv_w"], lane_pad(p["gdn_a_log"], HEADS),
              lane_pad(p["gdn_dt_bias"], HEADS), jnp.tile(p["gdn_norm_g"], HEADS).reshape(1, MIX_W)]
    gdn_specs = [slab_spec, pl.BlockSpec((CONV_K, ch), full), vec(LANES), vec(LANES), vec(MIX_W)]
    gdn_scratch = lambda: conv_scratch() + [pltpu.VMEM((MIX_W, MIX_W), F32)] + [tile() for _ in range(13)]
    ssd_in = [ssd_slab.reshape(bsz, seq, SLAB_W), p["ssd_conv_w"], p["ssd_conv_b"].reshape(1, ch),
              lane_pad(p["ssd_a_log"], 0), lane_pad(p["ssd_dt_bias"], 0),
              jnp.repeat(p["ssd_d"], HEAD_W).reshape(1, MIX_W), p["ssd_norm_g"].reshape(1, MIX_W)]
    ssd_specs = [slab_spec, pl.BlockSpec((CONV_K, ch), full), vec(ch), vec(LANES), vec(LANES), vec(MIX_W), vec(MIX_W)]
    ssd_scratch = lambda: (conv_scratch() + [pltpu.VMEM((2, SSD_N, SSD_N), F32)]
                           + [tile(MXU_DTYPE) for _ in range(4)] + [tile() for _ in range(4)])
    ml_in = [ml_slab.reshape(bsz, seq, SLAB_W),
             lane_pad(jnp.concatenate([p["mlstm_i_bias"], p["mlstm_f_bias"]]), 0), p["mlstm_norm_g"].reshape(1, MIX_W)]
    ml_specs = [slab_spec, vec(LANES), vec(MIX_W)]
    ml_scratch = lambda: ([pltpu.VMEM((MIX_W, 2 * MIX_W), F32), pltpu.VMEM((1, MIX_W), F32)]
                          + [tile() for _ in range(6)])
    assert (len(gdn_in), 1, len(gdn_scratch())) == _N_GDN_REFS
    assert (len(ssd_in), 1, len(ssd_scratch())) == _N_SSD_REFS
    assert (len(ml_in), 1, len(ml_scratch())) == _N_MLSTM_REFS
    scratch = []
    for _ in range(nb):
        scratch += gdn_scratch() + ssd_scratch() + ml_scratch()
    outs = pl.pallas_call(
        functools.partial(_mixers_kernel, t=t, nb=nb),
        out_shape=(out_shape, out_shape, out_shape),
        grid=(bsz // nb, seq // t),
        in_specs=gdn_specs + ssd_specs + ml_specs,
        out_specs=(out_spec, out_spec, out_spec),
        scratch_shapes=scratch,
        compiler_params=pltpu.CompilerParams(dimension_semantics=("parallel", "arbitrary"),
                                             vmem_limit_bytes=VMEM_LIMIT),
        name="recurrent_mixers",
    )(*gdn_in, *ssd_in, *ml_in)
    return [o.reshape(bsz * seq, MIX_W) for o in outs]


def _rope_tables(seq):
    inv_freq = ROPE_THETA ** (-jnp.arange(0, DIFF_DH, 2, dtype=F32) / DIFF_DH)
    ang = jnp.arange(seq, dtype=F32)[:, None] * inv_freq[None, :]
    cos, sin = jnp.cos(ang), jnp.sin(ang)
    reps = LANES // DIFF_DH
    cos_t = jnp.tile(jnp.concatenate([cos, cos], axis=-1), (1, reps))
    sin_t = jnp.tile(jnp.concatenate([-sin, sin], axis=-1), (1, reps))
    return cos_t, sin_t


def kernel(x, ffn1_pre_g, ffn1_w_gate, ffn1_w_up, ffn1_w_down, ffn1_post_g, mix_pre_g, w_in, gdn_conv_w, gdn_a_log, gdn_dt_bias, gdn_norm_g, diff_lam_q1, diff_lam_k1, diff_lam_q2, diff_lam_k2, diff_norm_g, ssd_conv_w, ssd_conv_b, ssd_a_log, ssd_dt_bias, ssd_d, ssd_norm_g, mlstm_i_bias, mlstm_f_bias, mlstm_norm_g, w_out, mix_post_g, ffn2_pre_g, ffn2_w_gate, ffn2_w_up, ffn2_w_down, ffn2_post_g):
    bsz, seq, d = x.shape
    x2 = x.reshape(bsz * seq, d)
    cos_t, sin_t = _rope_tables(seq)
    for l in range(DEPTH):
        lambda_init = 0.8 - 0.6 * math.exp(-0.3 * l)
        x2 = _ffn_call(x2, ffn1_pre_g[l], ffn1_w_gate[l], ffn1_w_up[l], ffn1_w_down[l], ffn1_post_g[l])
        gdn_slab, ssd_slab, ml_slab, dq, dk, dv = _proj_call(x2, mix_pre_g[l], w_in[l], cos_t, sin_t, seq)
        mixer_params = dict(
            gdn_conv_w=gdn_conv_w[l], gdn_a_log=gdn_a_log[l], gdn_dt_bias=gdn_dt_bias[l], gdn_norm_g=gdn_norm_g[l],
            ssd_conv_w=ssd_conv_w[l], ssd_conv_b=ssd_conv_b[l], ssd_a_log=ssd_a_log[l], ssd_dt_bias=ssd_dt_bias[l],
            ssd_d=ssd_d[l], ssd_norm_g=ssd_norm_g[l],
            mlstm_i_bias=mlstm_i_bias[l], mlstm_f_bias=mlstm_f_bias[l], mlstm_norm_g=mlstm_norm_g[l])
        out_a, out_c, out_d = _mixers_call(gdn_slab, ssd_slab, ml_slab, mixer_params, bsz, seq)
        lam_p = jnp.stack([diff_lam_q1[l], diff_lam_k1[l], diff_lam_q2[l], diff_lam_k2[l]])
        out_b = _attn_call(dq, dk, dv, lam_p, diff_norm_g[l], lambda_init, bsz, seq)
        x2 = _ffn_call(x2, ffn2_pre_g[l], ffn2_w_gate[l], ffn2_w_up[l], ffn2_w_down[l], ffn2_post_g[l],
                       mix=((out_a, out_b, out_c, out_d), w_out[l], mix_post_g[l]))
    return x2.reshape(bsz, seq, d)
```

```python
import functools
import math

import jax
import jax.numpy as jnp
from jax import lax
from jax.experimental import pallas as pl
from jax.experimental.pallas import tpu as pltpu

F32 = jnp.float32
MXU_DTYPE = jnp.bfloat16

D_MODEL = 1024
D_FF = 2816
DEPTH = 2
CHUNK = 64
CHUNK_SHIFT = 6
CONV_K = 4
EPS = 1e-6
ROPE_THETA = 10000.0
HEADS = 4
HEAD_W = 64
MIX_W = HEADS * HEAD_W
DIFF_DH = 32
DIFF_DH_SHIFT = 5
LOG2_E = math.log2(math.e)
SSD_N = 128
LANES = 128
NEG = -1e30

_GDN_COLS = (0, 1032)
_DIFF_COLS = (1032, 1800)
_SSD_COLS = (1800, 2828)
_MLSTM_COLS = (2828, 3860)
SLAB_W = 1152

FFN_TM = 512
FFN_FC = 256
MIX_T = 256
MIX_NB = 2
ATT_TQ = 256
ATT_NB = 2
ATT_RB = 16
ATT_TK = FFN_TM
VMEM_LIMIT = 56 * 1024 * 1024


def _dot(a, b):
    return jnp.dot(a.astype(MXU_DTYPE), b.astype(MXU_DTYPE), preferred_element_type=F32)


def _dot_nt(a, b):
    return lax.dot_general(a.astype(MXU_DTYPE), b.astype(MXU_DTYPE), (((1,), (1,)), ((), ())),
                           preferred_element_type=F32)


def _dot_tn(a, b):
    return lax.dot_general(a.astype(MXU_DTYPE), b.astype(MXU_DTYPE), (((0,), (0,)), ((), ())),
                           preferred_element_type=F32)


def _split3(x):
    x1 = x.astype(MXU_DTYPE)
    r1 = x - x1.astype(F32)
    x2 = r1.astype(MXU_DTYPE)
    return x1, x2, (r1 - x2.astype(F32)).astype(MXU_DTYPE)


def _dot_sel(sel, x, terms=3):
    sel = sel.astype(MXU_DTYPE)
    return sum(jnp.dot(sel, xi, preferred_element_type=F32) for xi in _split3(x)[:terms])


def _dot_xsel(x, sel, terms=3):
    sel = sel.astype(MXU_DTYPE)
    return sum(jnp.dot(xi, sel, preferred_element_type=F32) for xi in _split3(x)[:terms])


def _chunk_last(x):
    t, w = x.shape
    return jnp.concatenate([jnp.broadcast_to(x[c + CHUNK - 1:c + CHUNK, :], (CHUNK, w))
                            for c in range(0, t, CHUNK)], axis=0)


def _rms(x, g):
    return x * lax.rsqrt(jnp.mean(x * x, axis=-1, keepdims=True) + EPS) * g


def _silu(x):
    return x / (1.0 + jnp.exp(-x))


def _sigmoid(x):
    return 1.0 / (1.0 + jnp.exp(-x))


def _softplus(x):
    return jnp.maximum(x, 0.0) + jnp.log(1.0 + jnp.exp(-jnp.abs(x)))


def _iota(shape, dim):
    return lax.broadcasted_iota(jnp.int32, shape, dim)


def _chunk_consts(t):
    r = _iota((t, t), 0)
    c = _iota((t, t), 1)
    same = (r >> CHUNK_SHIFT) == (c >> CHUNK_SHIFT)
    ltile = jnp.where(same & (c <= r), 1.0, 0.0).astype(F32)
    csame = jnp.where(same, 1.0, 0.0).astype(F32)
    pos = _iota((t, MIX_W), 0) & (CHUNK - 1)
    m = _iota((t, MIX_W), 1) & (CHUNK - 1)
    return ltile, csame, pos, m


def _head_expand(first_lane):
    r = _iota((LANES, MIX_W), 0)
    c = _iota((LANES, MIX_W), 1)
    return jnp.where(r == (c >> CHUNK_SHIFT) + first_lane, 1.0, 0.0).astype(F32)


def _head_mask():
    return (_iota((MIX_W, MIX_W), 0) >> CHUNK_SHIFT) == (_iota((MIX_W, MIX_W), 1) >> CHUNK_SHIFT)


def _head_ones():
    return jnp.where(_head_mask(), 1.0, 0.0).astype(F32)


def _causal_conv(u, w_ref, carry_ref, buf_ref, t):
    buf_ref[0:8, :] = carry_ref[...]
    buf_ref[8:8 + t, :] = u
    carry_ref[...] = u[t - 8:t, :]
    acc = w_ref[CONV_K - 1:CONV_K, :] * u
    for j in range(CONV_K - 1):
        off = 8 - (CONV_K - 1) + j
        acc = acc + w_ref[j:j + 1, :] * buf_ref[off:off + t, :]
    return acc


def _ffn_body(x, pre_ref, wg_ref, wu_ref, wd_ref, post_ref, h_ref, acc_ref, fc):
    h_ref[...] = _rms(x, pre_ref[...]).astype(h_ref.dtype)
    acc_ref[...] = jnp.zeros_like(acc_ref)
    for c0 in range(0, wg_ref.shape[1], fc):
        g = jnp.dot(h_ref[...], wg_ref[:, c0:c0 + fc], preferred_element_type=F32)
        u = jnp.dot(h_ref[...], wu_ref[:, c0:c0 + fc], preferred_element_type=F32)
        a = (_silu(g) * u).astype(wd_ref.dtype)
        acc_ref[...] += jnp.dot(a, wd_ref[c0:c0 + fc, :], preferred_element_type=F32)
    return x + 0.5 * _rms(acc_ref[...], post_ref[...])


def _ffn_kernel(x_ref, pre_ref, wg_ref, wu_ref, wd_ref, post_ref, o_ref, h_ref, acc_ref, *, fc):
    o_ref[...] = _ffn_body(x_ref[...], pre_ref, wg_ref, wu_ref, wd_ref, post_ref, h_ref, acc_ref, fc)


def _outproj_ffn_kernel(x_ref, a_ref, b_ref, c_ref, d_ref, wo_ref, go_ref, pre_ref, wg_ref, wu_ref, wd_ref, post_ref,
                        o_ref, h_ref, acc_ref, *, fc):
    mixed = jnp.concatenate([a_ref[...], b_ref[...], c_ref[...], d_ref[...]], axis=-1).astype(wo_ref.dtype)
    x = x_ref[...] + _rms(jnp.dot(mixed, wo_ref[...], preferred_element_type=F32), go_ref[...])
    o_ref[...] = _ffn_body(x, pre_ref, wg_ref, wu_ref, wd_ref, post_ref, h_ref, acc_ref, fc)


def _resident(shape):
    return pl.BlockSpec(shape, lambda i: (0,) * len(shape), pipeline_mode=pl.Buffered(1))


def _ffn_call(x2, pre_g, w_gate, w_up, w_down, post_g, mix=None, tm=FFN_TM, fc=FFN_FC):
    m, d = x2.shape
    ff = w_gate.shape[1]
    row = lambda i: (i, 0)
    ffn_specs = [_resident((1, d)), _resident((d, ff)), _resident((d, ff)), _resident((ff, d)), _resident((1, d))]
    ffn_args = (pre_g.reshape(1, d), w_gate.astype(MXU_DTYPE), w_up.astype(MXU_DTYPE), w_down.astype(MXU_DTYPE),
                post_g.reshape(1, d))
    if mix is None:
        body, name = _ffn_kernel, "ffn"
        specs, args = ffn_specs, ffn_args
    else:
        outs, w_out, mix_post_g = mix
        body, name = _outproj_ffn_kernel, "mixer_out_proj_ffn"
        specs = [pl.BlockSpec((tm, MIX_W), row)] * 4 + [_resident((d, d)), _resident((1, d))] + ffn_specs
        args = (*outs, w_out.astype(MXU_DTYPE), mix_post_g.reshape(1, d)) + ffn_args
    return pl.pallas_call(
        functools.partial(body, fc=fc),
        out_shape=jax.ShapeDtypeStruct((m, d), F32),
        grid=(m // tm,),
        in_specs=[pl.BlockSpec((tm, d), row)] + specs,
        out_specs=pl.BlockSpec((tm, d), row),
        scratch_shapes=[pltpu.VMEM((tm, d), MXU_DTYPE), pltpu.VMEM((tm, d), F32)],
        compiler_params=pltpu.CompilerParams(dimension_semantics=("parallel",), vmem_limit_bytes=VMEM_LIMIT),
        name=name,
    )(x2, *args)


def _rope_half(x, cos, sin):
    lane = _iota(x.shape, 1)
    half = DIFF_DH // 2
    partner = jnp.where((lane & (DIFF_DH - 1)) < half, pltpu.roll(x, LANES - half, 1), pltpu.roll(x, half, 1))
    return x * cos + partner * sin


def _proj_kernel(x_ref, g_ref, wa_ref, wb_ref, wc_ref, wd_ref, cos_ref, sin_ref,
                 gdn_ref, ssd_ref, ml_ref, dq_ref, dk_ref, dv_ref):
    h = _rms(x_ref[...], g_ref[...]).astype(wa_ref.dtype)
    gdn_ref[...] = jnp.dot(h, wa_ref[...], preferred_element_type=F32)
    ssd_ref[...] = jnp.dot(h, wc_ref[...], preferred_element_type=F32)
    ml_ref[...] = jnp.dot(h, wd_ref[...], preferred_element_type=F32)
    d = jnp.dot(h, wb_ref[...], preferred_element_type=F32)
    cos = cos_ref[...]
    sin = sin_ref[...]
    scale = DIFF_DH ** -0.5 * LOG2_E
    for half in range(2):
        lo = half * LANES
        dq_ref[:, lo:lo + LANES] = (_rope_half(d[:, lo:lo + LANES], cos, sin) * scale).astype(dq_ref.dtype)
        dk_ref[:, lo:lo + LANES] = _rope_half(d[:, MIX_W + lo:MIX_W + lo + LANES], cos, sin).astype(dk_ref.dtype)
    ones = jnp.ones((d.shape[0], HEAD_W), F32)
    for h in range(HEADS):
        v_h = d[:, 2 * MIX_W + h * HEAD_W:2 * MIX_W + (h + 1) * HEAD_W]
        dv_ref[0, h * LANES:(h + 1) * LANES, :] = jnp.concatenate([v_h, ones], axis=-1).T.astype(dv_ref.dtype)


def _pad_cols(w, width):
    return jnp.pad(w, ((0, 0), (0, width - w.shape[1])))


def _proj_call(x2, pre_g, w_in, cos_t, sin_t, seq, tm=FFN_TM):
    m, d = x2.shape
    wa = _pad_cols(w_in[:, _GDN_COLS[0]:_GDN_COLS[1]], SLAB_W).astype(MXU_DTYPE)
    wb = w_in[:, _DIFF_COLS[0]:_DIFF_COLS[1]].astype(MXU_DTYPE)
    wc = _pad_cols(w_in[:, _SSD_COLS[0]:_SSD_COLS[1]], SLAB_W).astype(MXU_DTYPE)
    wd = _pad_cols(w_in[:, _MLSTM_COLS[0]:_MLSTM_COLS[1]], SLAB_W).astype(MXU_DTYPE)
    tiles_per_seq = seq // tm
    full = lambda i: (0, 0)
    row = lambda i: (i, 0)
    return pl.pallas_call(
        _proj_kernel,
        out_shape=(
            jax.ShapeDtypeStruct((m, SLAB_W), F32),
            jax.ShapeDtypeStruct((m, SLAB_W), F32),
            jax.ShapeDtypeStruct((m, SLAB_W), F32),
            jax.ShapeDtypeStruct((m, MIX_W), MXU_DTYPE),
            jax.ShapeDtypeStruct((m, MIX_W), MXU_DTYPE),
            jax.ShapeDtypeStruct((m // tm, HEADS * LANES, tm), MXU_DTYPE),
        ),
        grid=(m // tm,),
        in_specs=[
            pl.BlockSpec((tm, d), row),
            pl.BlockSpec((1, d), full),
            pl.BlockSpec((d, SLAB_W), full),
            pl.BlockSpec((d, 3 * MIX_W), full),
            pl.BlockSpec((d, SLAB_W), full),
            pl.BlockSpec((d, SLAB_W), full),
            pl.BlockSpec((tm, LANES), lambda i: (i % tiles_per_seq, 0)),
            pl.BlockSpec((tm, LANES), lambda i: (i % tiles_per_seq, 0)),
        ],
        out_specs=(
            pl.BlockSpec((tm, SLAB_W), row),
            pl.BlockSpec((tm, SLAB_W), row),
            pl.BlockSpec((tm, SLAB_W), row),
            pl.BlockSpec((tm, MIX_W), row),
            pl.BlockSpec((tm, MIX_W), row),
            pl.BlockSpec((1, HEADS * LANES, tm), lambda i: (i, 0, 0)),
        ),
        compiler_params=pltpu.CompilerParams(dimension_semantics=("parallel",), vmem_limit_bytes=VMEM_LIMIT),
        name="mixer_in_proj",
    )(x2, pre_g.reshape(1, d), wa, wb, wc, wd, cos_t, sin_t)


def _attn_kernel(q_ref, k_ref, vt_ref, lam_ref, g_ref, o_ref, qt_ref, m_ref, acc_ref, p_ref, s_ref, alpha_ref,
                 *, tq, tk, nb, kt_per_seq, lambda_init):
    i = pl.program_id(1)
    seqs = range(nb)

    @pl.when(i == 0)
    def _():
        qt_ref[...] = jnp.zeros_like(qt_ref)

    for b in seqs:
        q_t = q_ref[b].astype(F32).T
        for mp in range(2 * HEADS):
            dims = slice(mp * DIFF_DH, (mp + 1) * DIFF_DH)
            qt_ref[b, mp // 2, dims, (mp % 2) * tq:(mp % 2 + 1) * tq] = q_t[dims, :].astype(qt_ref.dtype)
    m_ref[...] = jnp.full_like(m_ref, NEG)
    acc_ref[...] = jnp.zeros_like(acc_ref)
    qpos = i * tq + (_iota((1, 2 * tq), 1) & (tq - 1))
    limit = ((qpos >> CHUNK_SHIFT) + 1) << CHUNK_SHIFT
    n_kt = (i * tq + tq + tk - 1) // tk

    def scores(b, h, j, rows=tk):
        kt = k_ref[b, pl.ds(pl.multiple_of(j * tk, tk), rows), :]
        s_ref[b, h, 0:rows, :] = jnp.dot(kt, qt_ref[b, h], preferred_element_type=F32)

    def softmax(b, h, j, rows, mask_from):
        def piece(r0):
            sp = s_ref[b, h, r0:r0 + ATT_RB, :]
            if mask_from is not None and r0 >= mask_from:
                key = j * tk + r0 + _iota((ATT_RB, 1), 0)
                sp = jnp.where(key < limit, sp, NEG)
            return sp

        mx = piece(0)
        for r0 in range(ATT_RB, rows, ATT_RB):
            mx = jnp.maximum(mx, piece(r0))
        m_old = m_ref[b, h]
        m_new = jnp.maximum(m_old, jnp.max(mx, axis=0, keepdims=True))
        for r0 in range(0, rows, ATT_RB):
            p_ref[b, h, r0:r0 + ATT_RB, :] = jnp.exp2(piece(r0) - m_new).astype(p_ref.dtype)
        alpha_ref[b, h] = jnp.exp2(m_old - m_new)
        m_ref[b, h] = m_new

    def values(b, h, j, rows):
        pv = jnp.dot(vt_ref[b * kt_per_seq + j, h * LANES:(h + 1) * LANES, 0:rows], p_ref[b, h, 0:rows, :],
                     preferred_element_type=F32)
        acc_ref[b, h] = alpha_ref[b, h] * acc_ref[b, h] + pv

    def step(j, rows, mask_from, prefetch_scores):
        for h in range(HEADS):
            for b in seqs:
                softmax(b, h, j, rows, mask_from)
                if h + 2 < HEADS:
                    scores(b, h + 2, j, rows)
                elif prefetch_scores:
                    scores(b, h + 2 - HEADS, j + 1)
                values(b, h, j, rows)

    def body(j, carry):
        step(j, tk, None, True)
        return carry

    for h in range(2):
        for b in seqs:
            scores(b, h, 0)
    lax.fori_loop(0, n_kt - 1, body, 0)
    last = n_kt - 1
    diagonal_first = (i * tq) == last * tk

    @pl.when(diagonal_first)
    def _():
        step(last, tq, 0, False)

    @pl.when(jnp.logical_not(diagonal_first))
    def _():
        step(last, tk, tq, False)

    lam_p = lam_ref[...]
    lam = (jnp.exp(jnp.sum(lam_p[0:1] * lam_p[1:2], axis=-1, keepdims=True))
           - jnp.exp(jnp.sum(lam_p[2:3] * lam_p[3:4], axis=-1, keepdims=True)) + lambda_init)
    for b in seqs:
        outs = []
        for h in range(HEADS):
            a = acc_ref[b, h]
            o = (a[:HEAD_W, :tq] / a[HEAD_W:HEAD_W + 1, :tq]
                 - lam * (a[:HEAD_W, tq:] / a[HEAD_W:HEAD_W + 1, tq:]))
            outs.append(o * lax.rsqrt(jnp.mean(o * o, axis=0, keepdims=True) + EPS))
        o_ref[b] = jnp.concatenate(outs, axis=0).T * g_ref[...] * (1.0 - lambda_init)


def _attn_call(dq, dk, dvt, lam_p, norm_g, lambda_init, bsz, seq, tq=ATT_TQ, tk=ATT_TK, nb=ATT_NB):
    assert tk == 2 * tq, "the last-key-tile split in _attn_kernel assumes tk == 2 tq"
    q3 = dq.reshape(bsz, seq, MIX_W)
    k3 = dk.reshape(bsz, seq, MIX_W)
    n_kt = seq // tk
    out = pl.pallas_call(
        functools.partial(_attn_kernel, tq=tq, tk=tk, nb=nb, kt_per_seq=n_kt, lambda_init=lambda_init),
        out_shape=jax.ShapeDtypeStruct((bsz, seq, MIX_W), F32),
        grid=(bsz // nb, seq // tq),
        in_specs=[
            pl.BlockSpec((nb, tq, MIX_W), lambda b, i: (b, i, 0)),
            pl.BlockSpec((nb, seq, MIX_W), lambda b, i: (b, 0, 0)),
            pl.BlockSpec((nb * n_kt, HEADS * LANES, tk), lambda b, i: (b, 0, 0)),
            pl.BlockSpec((4, DIFF_DH), lambda b, i: (0, 0)),
            pl.BlockSpec((1, MIX_W), lambda b, i: (0, 0)),
        ],
        out_specs=pl.BlockSpec((nb, tq, MIX_W), lambda b, i: (b, i, 0)),
        scratch_shapes=[
            pltpu.VMEM((nb, HEADS, MIX_W, 2 * tq), MXU_DTYPE),
            pltpu.VMEM((nb, HEADS, 1, 2 * tq), F32),
            pltpu.VMEM((nb, HEADS, LANES, 2 * tq), F32),
            pltpu.VMEM((nb, HEADS, tk, 2 * tq), MXU_DTYPE),
            pltpu.VMEM((nb, HEADS, tk, 2 * tq), F32),
            pltpu.VMEM((nb, HEADS, 1, 2 * tq), F32),
        ],
        compiler_params=pltpu.CompilerParams(dimension_semantics=("parallel", "arbitrary"),
                                             vmem_limit_bytes=VMEM_LIMIT),
        name="diff_attention",
    )(q3, k3, dvt, lam_p, jnp.tile(norm_g, HEADS).reshape(1, MIX_W))
    return out.reshape(bsz * seq, MIX_W)


def _head_block(x, h):
    blk = x[:, (h // 2) * LANES:(h // 2 + 1) * LANES]
    own = (_iota(blk.shape, 1) >> CHUNK_SHIFT) == (h % 2)
    return jnp.where(own, blk, jnp.zeros_like(blk))


def _bd_rows(blocks):
    zero = jnp.zeros_like(blocks[0])
    return jnp.concatenate([jnp.concatenate([b, zero] if h < 2 else [zero, b], axis=1)
                            for h, b in enumerate(blocks)], axis=0)


def _bd(x):
    return _bd_rows([_head_block(x, h) for h in range(HEADS)])


def _bd_keep(m):
    return _bd_rows([_head_block(m[h * CHUNK:(h + 1) * CHUNK, :], h) for h in range(HEADS)])


def _split(x):
    hi = x.astype(MXU_DTYPE)
    return hi, (x - hi.astype(F32)).astype(MXU_DTYPE)


def _bd_matmul3(a, b):
    ah, al = _split(a)
    bh, bl = _split(b)
    m = a.shape[0]
    r = jnp.dot(jnp.concatenate([ah, al], axis=0), _bd(bh), preferred_element_type=F32)
    return r[:m] + r[m:] + jnp.dot(ah, _bd(bl), preferred_element_type=F32)


def _chunk_slices(t):
    return [slice(c * CHUNK, (c + 1) * CHUNK) for c in range(t // CHUNK)]


def _ssd_steps(slab_ref, cw_ref, cb_ref, alog_ref, dtb_ref, dskip_ref, g_ref, y_ref,
               carry_ref, buf_ref, st_ref, xc_ref, xcd_ref, bm_ref, cm_ref, lm_ref, ea_ref, el_ref, yo_ref, t, b):
    ltile, _, pos, mm = _chunk_consts(t)
    expand = _head_expand(0)
    xbc = _silu(_causal_conv(slab_ref[b,:, MIX_W:4 * MIX_W], cw_ref, carry_ref, buf_ref, t) + cb_ref[...])
    x = xbc[:, 0:MIX_W]
    bm_ref[...] = xbc[:, MIX_W:2 * MIX_W].astype(bm_ref.dtype)
    cm_ref[...] = xbc[:, 2 * MIX_W:3 * MIX_W].astype(cm_ref.dtype)
    yield
    dt = _softplus(slab_ref[b,:, 4 * MIX_W:] + dtb_ref[...])
    da = dt * (-jnp.exp(alog_ref[...]))
    dt_e = _dot_xsel(dt, expand)
    da_e = _dot_xsel(da, expand)
    yield
    acs_e = _dot_sel(ltile, da_e)
    acs_last_e = _chunk_last(acs_e)
    ldiff = _dot_sel(ltile, jnp.where(pos > mm, da_e, 0.0))
    lm_ref[...] = jnp.where(mm <= pos, jnp.exp(ldiff), 0.0)
    xc = x * dt_e
    xc_ref[...] = xc.astype(xc_ref.dtype)
    xcd_ref[...] = (xc * jnp.exp(acs_last_e - acs_e)).astype(xcd_ref.dtype)
    ea_ref[...] = jnp.exp(acs_e)
    el_ref[...] = jnp.exp(acs_last_e)
    yield
    for c in range(t // CHUNK):
        r0 = c * CHUNK
        rows = slice(r0, r0 + CHUNK)
        for grp in range(2):
            gl = slice(grp * SSD_N, (grp + 1) * SSD_N)
            bg = bm_ref[rows, gl]
            cg = cm_ref[rows, gl]
            cb = _dot_nt(cg, bg)
            ys = []
            for hh in range(2):
                hl = slice(grp * SSD_N + hh * HEAD_W, grp * SSD_N + (hh + 1) * HEAD_W)
                ys.append(_dot(cb * lm_ref[rows, hl], xc_ref[rows, hl]))
            y_diag = jnp.concatenate(ys, axis=-1)
            st = st_ref[grp]
            y_off = _dot(cg, st) * ea_ref[rows, gl]
            st_ref[grp] = st * el_ref[r0:r0 + 1, gl] + _dot_tn(bg, xcd_ref[rows, gl])
            yo_ref[rows, gl] = y_diag + y_off
        yield
    y = (yo_ref[...] + x * dskip_ref[...]) * _silu(slab_ref[b,:, 0:MIX_W])
    gs = g_ref[...]
    for grp in range(2):
        gl = slice(grp * SSD_N, (grp + 1) * SSD_N)
        y_ref[b, :, gl] = _rms(y[:, gl], gs[:, gl])


def _gdn_steps(slab_ref, cw_ref, alog_ref, dtb_ref, g_ref, y_ref,
               carry_ref, buf_ref, st_ref, qn_ref, kn_ref, qd_ref, kd_ref, kb_ref, vb_ref, kbg_ref, dec_ref,
               egl_ref, u_ref, w_ref, attn_ref, o_ref, t, b):
    ltile, _, pos, mm = _chunk_consts(t)
    ones_h = _head_ones()
    qkv = _silu(_causal_conv(slab_ref[b,:, 0:3 * MIX_W], cw_ref, carry_ref, buf_ref, t))
    q = qkv[:, 0:MIX_W]
    k = qkv[:, MIX_W:2 * MIX_W]
    v = qkv[:, 2 * MIX_W:]
    yield
    qn = q * lax.rsqrt(_dot_xsel(q * q, ones_h, terms=2) + EPS) * (HEAD_W ** -0.5)
    kn = k * lax.rsqrt(_dot_xsel(k * k, ones_h, terms=2) + EPS)
    yield
    ba = slab_ref[b,:, 4 * MIX_W:]
    beta_e = _dot_xsel(_sigmoid(ba), _head_expand(0))
    g4 = -jnp.exp(alog_ref[...]) * _softplus(ba + dtb_ref[...])
    g_e = _dot_xsel(g4, _head_expand(HEADS))
    yield
    gc_e = _dot_sel(ltile, g_e)
    gl_e = _chunk_last(gc_e)
    diff = _dot_sel(ltile, jnp.where(pos > mm, g_e, 0.0))
    dec_ref[...] = jnp.where(mm <= pos, jnp.exp(diff), 0.0)
    egc = jnp.exp(gc_e)
    kb = kn * beta_e
    qn_ref[...] = qn
    kn_ref[...] = kn
    qd_ref[...] = qn * egc
    kd_ref[...] = kn * jnp.exp(gl_e - gc_e)
    kb_ref[...] = kb
    vb_ref[...] = v * beta_e
    kbg_ref[...] = kb * egc
    egl_ref[...] = jnp.exp(gl_e)
    yield

    r_ss = _iota((CHUNK, MIX_W), 0)
    m_ss = _iota((CHUNK, MIX_W), 1) & (CHUNK - 1)
    eye_ss = jnp.where(r_ss == m_ss, 1.0, 0.0).astype(F32)
    chunk_rows = _chunk_slices(t)

    ys, tms = [], []
    for rows in chunk_rows:
        lhs = jnp.concatenate([kb_ref[rows, :], qn_ref[rows, :]], axis=0).astype(MXU_DTYPE)
        raw = lax.dot_general(lhs, _bd(kn_ref[rows, :].astype(MXU_DTYPE)), (((1,), (1,)), ((), ())),
                              preferred_element_type=F32)
        dec = dec_ref[rows, :]
        attn_ref[rows, :] = raw[CHUNK:] * dec
        y = jnp.where(m_ss < r_ss, -(raw[:CHUNK] * dec), 0.0)
        ys.append(y)
        tms.append(eye_ss + y)
    yield
    ys = [_bd_matmul3(y, y) for y in ys]
    yield
    for _ in range(4):
        rs = [_bd_matmul3(jnp.concatenate([tm, y], axis=0), y) for tm, y in zip(tms, ys)]
        tms = [tm + r[:CHUNK] for tm, r in zip(tms, rs)]
        ys = [r[CHUNK:] for r in rs]
        yield
    tms = [tm + _bd_matmul3(tm, y) for tm, y in zip(tms, ys)]
    yield
    for rows, tm in zip(chunk_rows, tms):
        rhs = jnp.concatenate([_bd(vb_ref[rows, :].astype(MXU_DTYPE)),
                               _bd(kbg_ref[rows, :].astype(MXU_DTYPE))], axis=1)
        uw = jnp.dot(tm.astype(MXU_DTYPE), rhs, preferred_element_type=F32)
        u_ref[rows, :] = uw[:, :MIX_W]
        w_ref[rows, :] = uw[:, MIX_W:]
    yield

    st = st_ref[...]
    for c, rows in enumerate(chunk_rows):
        lhs = jnp.concatenate([w_ref[rows, :], qd_ref[rows, :]], axis=0).astype(MXU_DTYPE)
        r = jnp.dot(lhs, st.astype(MXU_DTYPE), preferred_element_type=F32)
        v_new = u_ref[rows, :] - r[:CHUNK]
        v_new_c = v_new.astype(MXU_DTYPE)
        o_ref[rows, :] = r[CHUNK:] + jnp.dot(attn_ref[rows, :].astype(MXU_DTYPE), _bd(v_new_c),
                                             preferred_element_type=F32)
        kv = lax.dot_general(kd_ref[rows, :].astype(MXU_DTYPE), v_new_c, (((0,), (0,)), ((), ())),
                             preferred_element_type=F32)
        st = st * egl_ref[c * CHUNK:c * CHUNK + 1, :] + _bd_keep(kv)
        yield
    st_ref[...] = st

    o = o_ref[...]
    o = o * lax.rsqrt(_dot_xsel(o * o, ones_h, terms=2) * (1.0 / HEAD_W) + EPS) * g_ref[...]
    y_ref[b] =o * _silu(slab_ref[b,:, 3 * MIX_W:4 * MIX_W])


def _mlstm_steps(slab_ref, bias_ref, g_ref, y_ref, st_ref, m_ref, dl_ref, b_ref, ge_ref, dmax_ref, qk_ref,
                 o_ref, t, b):
    ltile, csame, pos, mm = _chunk_consts(t)
    gates = slab_ref[b,:, 4 * MIX_W:] + bias_ref[...]
    li_e = _dot_xsel(gates, _head_expand(0))
    lf_e = _dot_xsel(-_softplus(-gates), _head_expand(HEADS))
    yield
    b_e = _dot_sel(ltile, lf_e)
    dlog = _dot_sel(ltile, jnp.where(pos > mm, lf_e, 0.0)) + _dot_sel(csame, jnp.where(pos == mm, li_e, 0.0))
    dl = jnp.where(mm <= pos, dlog, NEG)
    yield
    lane_head = _iota((t, MIX_W), 1) >> CHUNK_SHIFT
    dmax = jnp.full((t, MIX_W), NEG, F32)
    for h in range(HEADS):
        in_head = lane_head == h
        dmax = jnp.where(in_head, jnp.max(jnp.where(in_head, dl, NEG), axis=-1, keepdims=True), dmax)
    dl_ref[...] = dl
    dmax_ref[...] = dmax
    b_ref[...] = b_e
    ge_ref[...] = _chunk_last(b_e) - b_e + li_e
    yield

    ones_bd = _head_ones().astype(MXU_DTYPE)
    ones_rows = jnp.ones((CHUNK, MIX_W), MXU_DTYPE)
    chunk_rows = _chunk_slices(t)
    k_scale = HEAD_W ** -0.5
    for rows in chunk_rows:
        k_c = (slab_ref[b,rows, MIX_W:2 * MIX_W] * k_scale).astype(MXU_DTYPE)
        qk_ref[rows, :] = lax.dot_general(slab_ref[b,rows, 0:MIX_W].astype(MXU_DTYPE), _bd(k_c),
                                          (((1,), (1,)), ((), ())), preferred_element_type=F32)
    yield

    st = st_ref[...]
    m_st = m_ref[...]
    for rows in chunk_rows:
        q_c = slab_ref[b,rows, 0:MIX_W].astype(MXU_DTYPE)
        k_c = slab_ref[b,rows, MIX_W:2 * MIX_W] * k_scale
        v_c = slab_ref[b,rows, 2 * MIX_W:3 * MIX_W].astype(MXU_DTYPE)
        b_c = b_ref[rows, :]
        m_t = jnp.maximum(b_c + m_st, dmax_ref[rows, :])
        w_st = jnp.exp(b_c + m_st - m_t)
        s = qk_ref[rows, :] * jnp.exp(dl_ref[rows, :] - m_t)
        qs = jnp.dot(q_c, st.astype(MXU_DTYPE), preferred_element_type=F32)
        sv = jnp.dot(s.astype(MXU_DTYPE), jnp.concatenate([_bd(v_c), ones_bd], axis=1),
                     preferred_element_type=F32)
        num = w_st * qs[:, :MIX_W] + sv[:, :MIX_W]
        den = w_st * qs[:, MIX_W:] + sv[:, MIX_W:]
        o_ref[rows, :] = num / jnp.maximum(jnp.abs(den), jnp.exp(-m_t))
        ge_c = ge_ref[rows, :]
        b_last = b_c[CHUNK - 1:CHUNK, :]
        m_new = jnp.maximum(b_last + m_st, jnp.max(ge_c, axis=0, keepdims=True))
        w_old = jnp.exp(b_last + m_st - m_new)
        k_w = (k_c * jnp.exp(ge_c - m_new)).astype(MXU_DTYPE)
        upd = lax.dot_general(k_w, jnp.concatenate([v_c, ones_rows], axis=1), (((0,), (0,)), ((), ())),
                              preferred_element_type=F32)
        st = (st * jnp.concatenate([w_old, w_old], axis=1)
              + jnp.concatenate([_bd_keep(upd[:, :MIX_W]), _bd_keep(upd[:, MIX_W:])], axis=1))
        m_st = m_new
        yield
    st_ref[...] = st
    m_ref[...] = m_st

    hcat = o_ref[...] * _sigmoid(slab_ref[b,:, 3 * MIX_W:4 * MIX_W])
    ms = _dot_xsel(hcat * hcat, _head_ones(), terms=2) * (1.0 / HEAD_W)
    y_ref[b] =hcat * lax.rsqrt(ms + EPS) * g_ref[...]


_N_GDN_REFS = (5, 1, 16)
_N_SSD_REFS = (7, 1, 11)
_N_MLSTM_REFS = (3, 1, 8)


def _mixers_kernel(*refs, t, nb):
    counts = (_N_GDN_REFS, _N_SSD_REFS, _N_MLSTM_REFS)
    pos = 0
    ins, outs = [], []
    for c in counts:
        ins.append(refs[pos:pos + c[0]])
        pos += c[0]
    for c in counts:
        outs.append(refs[pos:pos + c[1]])
        pos += c[1]
    scratch = []
    for b in range(nb):
        per_mixer = []
        for c in counts:
            per_mixer.append(refs[pos:pos + c[2]])
            pos += c[2]
        scratch.append(per_mixer)

    @pl.when(pl.program_id(1) == 0)
    def _():
        for b in range(nb):
            gdn_s, ssd_s, ml_s = scratch[b]
            for state_ref in (gdn_s[0], gdn_s[2], ssd_s[0], ssd_s[2], ml_s[0], ml_s[1]):
                state_ref[...] = jnp.zeros_like(state_ref)

    gdn = [_gdn_steps(*ins[0], *outs[0], *scratch[b][0], t, b) for b in range(nb)]
    ssd = [_ssd_steps(*ins[1], *outs[1], *scratch[b][1], t, b) for b in range(nb)]
    mls = [_mlstm_steps(*ins[2], *outs[2], *scratch[b][2], t, b) for b in range(nb)]
    streams = gdn + gdn + ssd + mls
    while streams:
        for stream in list(streams):
            if stream in streams and next(stream, StopIteration) is StopIteration:
                streams = [s for s in streams if s is not stream]


def _mixers_call(gdn_slab, ssd_slab, ml_slab, p, bsz, seq, t=MIX_T, nb=MIX_NB):
    ch = 3 * MIX_W
    lane_pad = lambda v, before: jnp.pad(v, (before, LANES - before - v.shape[0])).reshape(1, LANES)
    full = lambda b, i: (0, 0)
    slab_spec = pl.BlockSpec((nb, t, SLAB_W), lambda b, i: (b, i, 0))
    vec = lambda n: pl.BlockSpec((1, n), full)
    tile = lambda dtype=F32: pltpu.VMEM((t, MIX_W), dtype)
    out_spec = pl.BlockSpec((nb, t, MIX_W), lambda b, i: (b, i, 0))
    out_shape = jax.ShapeDtypeStruct((bsz, seq, MIX_W), F32)
    conv_scratch = lambda: [pltpu.VMEM((8, ch), F32), pltpu.VMEM((t + 8, ch), F32)]
    gdn_in = [gdn_slab.reshape(bsz, seq, SLAB_W), p["gdn_conv_w"], lane_pad(p["gdn_a_log"], HEADS),
              lane_pad(p["gdn_dt_bias"], HEADS), jnp.tile(p["gdn_norm_g"], HEADS).reshape(1, MIX_W)]
    gdn_specs = [slab_spec, pl.BlockSpec((CONV_K, ch), full), vec(LANES), vec(LANES), vec(MIX_W)]
    gdn_scratch = lambda: conv_scratch() + [pltpu.VMEM((MIX_W, MIX_W), F32)] + [tile() for _ in range(13)]
    ssd_in = [ssd_slab.reshape(bsz, seq, SLAB_W), p["ssd_conv_w"], p["ssd_conv_b"].reshape(1, ch),
              lane_pad(p["ssd_a_log"], 0), lane_pad(p["ssd_dt_bias"], 0),
              jnp.repeat(p["ssd_d"], HEAD_W).reshape(1, MIX_W), p["ssd_norm_g"].reshape(1, MIX_W)]
    ssd_specs = [slab_spec, pl.BlockSpec((CONV_K, ch), full), vec(ch), vec(LANES), vec(LANES), vec(MIX_W), vec(MIX_W)]
    ssd_scratch = lambda: (conv_scratch() + [pltpu.VMEM((2, SSD_N, SSD_N), F32)]
                           + [tile(MXU_DTYPE) for _ in range(4)] + [tile() for _ in range(4)])
    ml_in = [ml_slab.reshape(bsz, seq, SLAB_W),
             lane_pad(jnp.concatenate([p["mlstm_i_bias"], p["mlstm_f_bias"]]), 0), p["mlstm_norm_g"].reshape(1, MIX_W)]
    ml_specs = [slab_spec, vec(LANES), vec(MIX_W)]
    ml_scratch = lambda: ([pltpu.VMEM((MIX_W, 2 * MIX_W), F32), pltpu.VMEM((1, MIX_W), F32)]
                          + [tile() for _ in range(6)])
    assert (len(gdn_in), 1, len(gdn_scratch())) == _N_GDN_REFS
    assert (len(ssd_in), 1, len(ssd_scratch())) == _N_SSD_REFS
    assert (len(ml_in), 1, len(ml_scratch())) == _N_MLSTM_REFS
    scratch = []
    for _ in range(nb):
        scratch += gdn_scratch() + ssd_scratch() + ml_scratch()
    outs = pl.pallas_call(
        functools.partial(_mixers_kernel, t=t, nb=nb),
        out_shape=(out_shape, out_shape, out_shape),
        grid=(bsz // nb, seq // t),
        in_specs=gdn_specs + ssd_specs + ml_specs,
        out_specs=(out_spec, out_spec, out_spec),
        scratch_shapes=scratch,
        compiler_params=pltpu.CompilerParams(dimension_semantics=("parallel", "arbitrary"),
                                             vmem_limit_bytes=VMEM_LIMIT),
        name="recurrent_mixers",
    )(*gdn_in, *ssd_in, *ml_in)
    return [o.reshape(bsz * seq, MIX_W) for o in outs]


def _rope_tables(seq):
    inv_freq = ROPE_THETA ** (-jnp.arange(0, DIFF_DH, 2, dtype=F32) / DIFF_DH)
    ang = jnp.arange(seq, dtype=F32)[:, None] * inv_freq[None, :]
    cos, sin = jnp.cos(ang), jnp.sin(ang)
    reps = LANES // DIFF_DH
    cos_t = jnp.tile(jnp.concatenate([cos, cos], axis=-1), (1, reps))
    sin_t = jnp.tile(jnp.concatenate([-sin, sin], axis=-1), (1, reps))
    return cos_t, sin_t


def kernel(x, ffn1_pre_g, ffn1_w_gate, ffn1_w_up, ffn1_w_down, ffn1_post_g, mix_pre_g, w_in, gdn_conv_w, gdn_a_log, gdn_dt_bias, gdn_norm_g, diff_lam_q1, diff_lam_k1, diff_lam_q2, diff_lam_k2, diff_norm_g, ssd_conv_w, ssd_conv_b, ssd_a_log, ssd_dt_bias, ssd_d, ssd_norm_g, mlstm_i_bias, mlstm_f_bias, mlstm_norm_g, w_out, mix_post_g, ffn2_pre_g, ffn2_w_gate, ffn2_w_up, ffn2_w_down, ffn2_post_g):
    bsz, seq, d = x.shape
    x2 = x.reshape(bsz * seq, d)
    cos_t, sin_t = _rope_tables(seq)
    for l in range(DEPTH):
        lambda_init = 0.8 - 0.6 * math.exp(-0.3 * l)
        x2 = _ffn_call(x2, ffn1_pre_g[l], ffn1_w_gate[l], ffn1_w_up[l], ffn1_w_down[l], ffn1_post_g[l])
        gdn_slab, ssd_slab, ml_slab, dq, dk, dv = _proj_call(x2, mix_pre_g[l], w_in[l], cos_t, sin_t, seq)
        mixer_params = dict(
            gdn_conv_w=gdn_conv_w[l], gdn_a_log=gdn_a_log[l], gdn_dt_bias=gdn_dt_bias[l], gdn_norm_g=gdn_norm_g[l],
            ssd_conv_w=ssd_conv_w[l], ssd_conv_b=ssd_conv_b[l], ssd_a_log=ssd_a_log[l], ssd_dt_bias=ssd_dt_bias[l],
            ssd_d=ssd_d[l], ssd_norm_g=ssd_norm_g[l],
            mlstm_i_bias=mlstm_i_bias[l], mlstm_f_bias=mlstm_f_bias[l], mlstm_norm_g=mlstm_norm_g[l])
        out_a, out_c, out_d = _mixers_call(gdn_slab, ssd_slab, ml_slab, mixer_params, bsz, seq)
        lam_p = jnp.stack([diff_lam_q1[l], diff_lam_k1[l], diff_lam_q2[l], diff_lam_k2[l]])
        out_b = _attn_call(dq, dk, dv, lam_p, diff_norm_g[l], lambda_init, bsz, seq)
        x2 = _ffn_call(x2, ffn2_pre_g[l], ffn2_w_gate[l], ffn2_w_up[l], ffn2_w_down[l], ffn2_post_g[l],
                       mix=((out_a, out_b, out_c, out_d), w_out[l], mix_post_g[l]))
    return x2.reshape(bsz, seq, d)
```

```python
import functools
import math

import jax
import jax.numpy as jnp
from jax import lax
from jax.experimental import pallas as pl
from jax.experimental.pallas import tpu as pltpu

F32 = jnp.float32
MXU_DTYPE = jnp.bfloat16

D_MODEL = 1024
D_FF = 2816
DEPTH = 2
CHUNK = 64
CHUNK_SHIFT = 6
CONV_K = 4
EPS = 1e-6
ROPE_THETA = 10000.0
HEADS = 4
HEAD_W = 64
MIX_W = HEADS * HEAD_W
DIFF_DH = 32
DIFF_DH_SHIFT = 5
LOG2_E = math.log2(math.e)
SSD_N = 128
LANES = 128
NEG = -1e30

_GDN_COLS = (0, 1032)
_DIFF_COLS = (1032, 1800)
_SSD_COLS = (1800, 2828)
_MLSTM_COLS = (2828, 3860)
SLAB_W = 1152

FFN_TM = 512
FFN_FC = 256
MIX_T = 256
MIX_NB = 2
ATT_TQ = 256
ATT_NB = 2
ATT_RB = 16
ATT_TK = FFN_TM
VMEM_LIMIT = 56 * 1024 * 1024


def _dot(a, b):
    return jnp.dot(a.astype(MXU_DTYPE), b.astype(MXU_DTYPE), preferred_element_type=F32)


def _dot_nt(a, b):
    return lax.dot_general(a.astype(MXU_DTYPE), b.astype(MXU_DTYPE), (((1,), (1,)), ((), ())),
                           preferred_element_type=F32)


def _dot_tn(a, b):
    return lax.dot_general(a.astype(MXU_DTYPE), b.astype(MXU_DTYPE), (((0,), (0,)), ((), ())),
                           preferred_element_type=F32)


def _split3(x):
    x1 = x.astype(MXU_DTYPE)
    r1 = x - x1.astype(F32)
    x2 = r1.astype(MXU_DTYPE)
    return x1, x2, (r1 - x2.astype(F32)).astype(MXU_DTYPE)


def _dot_sel(sel, x, terms=3):
    sel = sel.astype(MXU_DTYPE)
    return sum(jnp.dot(sel, xi, preferred_element_type=F32) for xi in _split3(x)[:terms])


def _dot_xsel(x, sel, terms=3):
    sel = sel.astype(MXU_DTYPE)
    return sum(jnp.dot(xi, sel, preferred_element_type=F32) for xi in _split3(x)[:terms])


def _chunk_last(x):
    t, w = x.shape
    return jnp.concatenate([jnp.broadcast_to(x[c + CHUNK - 1:c + CHUNK, :], (CHUNK, w))
                            for c in range(0, t, CHUNK)], axis=0)


def _rms(x, g):
    return x * lax.rsqrt(jnp.mean(x * x, axis=-1, keepdims=True) + EPS) * g


def _silu(x):
    return x / (1.0 + jnp.exp(-x))


def _sigmoid(x):
    return 1.0 / (1.0 + jnp.exp(-x))


def _softplus(x):
    return jnp.maximum(x, 0.0) + jnp.log(1.0 + jnp.exp(-jnp.abs(x)))


def _iota(shape, dim):
    return lax.broadcasted_iota(jnp.int32, shape, dim)


def _chunk_consts(t):
    r = _iota((t, t), 0)
    c = _iota((t, t), 1)
    same = (r >> CHUNK_SHIFT) == (c >> CHUNK_SHIFT)
    ltile = jnp.where(same & (c <= r), 1.0, 0.0).astype(F32)
    csame = jnp.where(same, 1.0, 0.0).astype(F32)
    pos = _iota((t, MIX_W), 0) & (CHUNK - 1)
    m = _iota((t, MIX_W), 1) & (CHUNK - 1)
    return ltile, csame, pos, m


def _head_expand(first_lane):
    r = _iota((LANES, MIX_W), 0)
    c = _iota((LANES, MIX_W), 1)
    return jnp.where(r == (c >> CHUNK_SHIFT) + first_lane, 1.0, 0.0).astype(F32)


def _head_mask():
    return (_iota((MIX_W, MIX_W), 0) >> CHUNK_SHIFT) == (_iota((MIX_W, MIX_W), 1) >> CHUNK_SHIFT)


def _head_ones():
    return jnp.where(_head_mask(), 1.0, 0.0).astype(F32)


def _causal_conv(u, w_ref, carry_ref, buf_ref, t):
    buf_ref[0:8, :] = carry_ref[...]
    buf_ref[8:8 + t, :] = u
    carry_ref[...] = u[t - 8:t, :]
    acc = w_ref[CONV_K - 1:CONV_K, :] * u
    for j in range(CONV_K - 1):
        off = 8 - (CONV_K - 1) + j
        acc = acc + w_ref[j:j + 1, :] * buf_ref[off:off + t, :]
    return acc


def _ffn_body(x, pre_ref, wg_ref, wu_ref, wd_ref, post_ref, h_ref, acc_ref, fc):
    h_ref[...] = _rms(x, pre_ref[...]).astype(h_ref.dtype)
    acc_ref[...] = jnp.zeros_like(acc_ref)
    for c0 in range(0, wg_ref.shape[1], fc):
        g = jnp.dot(h_ref[...], wg_ref[:, c0:c0 + fc], preferred_element_type=F32)
        u = jnp.dot(h_ref[...], wu_ref[:, c0:c0 + fc], preferred_element_type=F32)
        a = (_silu(g) * u).astype(wd_ref.dtype)
        acc_ref[...] += jnp.dot(a, wd_ref[c0:c0 + fc, :], preferred_element_type=F32)
    return x + 0.5 * _rms(acc_ref[...], post_ref[...])


def _ffn_kernel(x_ref, pre_ref, wg_ref, wu_ref, wd_ref, post_ref, o_ref, h_ref, acc_ref, *, fc):
    o_ref[...] = _ffn_body(x_ref[...], pre_ref, wg_ref, wu_ref, wd_ref, post_ref, h_ref, acc_ref, fc)


def _outproj_ffn_kernel(x_ref, a_ref, b_ref, c_ref, d_ref, wo_ref, go_ref, pre_ref, wg_ref, wu_ref, wd_ref, post_ref,
                        o_ref, h_ref, acc_ref, *, fc):
    mixed = jnp.concatenate([a_ref[...], b_ref[...], c_ref[...], d_ref[...]], axis=-1).astype(wo_ref.dtype)
    x = x_ref[...] + _rms(jnp.dot(mixed, wo_ref[...], preferred_element_type=F32), go_ref[...])
    o_ref[...] = _ffn_body(x, pre_ref, wg_ref, wu_ref, wd_ref, post_ref, h_ref, acc_ref, fc)


def _resident(shape):
    return pl.BlockSpec(shape, lambda i: (0,) * len(shape), pipeline_mode=pl.Buffered(1))


def _ffn_call(x2, pre_g, w_gate, w_up, w_down, post_g, mix=None, tm=FFN_TM, fc=FFN_FC):
    m, d = x2.shape
    ff = w_gate.shape[1]
    row = lambda i: (i, 0)
    ffn_specs = [_resident((1, d)), _resident((d, ff)), _resident((d, ff)), _resident((ff, d)), _resident((1, d))]
    ffn_args = (pre_g.reshape(1, d), w_gate.astype(MXU_DTYPE), w_up.astype(MXU_DTYPE), w_down.astype(MXU_DTYPE),
                post_g.reshape(1, d))
    if mix is None:
        body, name = _ffn_kernel, "ffn"
        specs, args = ffn_specs, ffn_args
    else:
        outs, w_out, mix_post_g = mix
        body, name = _outproj_ffn_kernel, "mixer_out_proj_ffn"
        specs = [pl.BlockSpec((tm, MIX_W), row)] * 4 + [_resident((d, d)), _resident((1, d))] + ffn_specs
        args = (*outs, w_out.astype(MXU_DTYPE), mix_post_g.reshape(1, d)) + ffn_args
    return pl.pallas_call(
        functools.partial(body, fc=fc),
        out_shape=jax.ShapeDtypeStruct((m, d), F32),
        grid=(m // tm,),
        in_specs=[pl.BlockSpec((tm, d), row)] + specs,
        out_specs=pl.BlockSpec((tm, d), row),
        scratch_shapes=[pltpu.VMEM((tm, d), MXU_DTYPE), pltpu.VMEM((tm, d), F32)],
        compiler_params=pltpu.CompilerParams(dimension_semantics=("parallel",), vmem_limit_bytes=VMEM_LIMIT),
        name=name,
    )(x2, *args)


def _rope_half(x, cos, sin):
    lane = _iota(x.shape, 1)
    half = DIFF_DH // 2
    partner = jnp.where((lane & (DIFF_DH - 1)) < half, pltpu.roll(x, LANES - half, 1), pltpu.roll(x, half, 1))
    return x * cos + partner * sin


def _proj_kernel(x_ref, g_ref, wa_ref, wb_ref, wc_ref, wd_ref, cos_ref, sin_ref,
                 gdn_ref, ssd_ref, ml_ref, dq_ref, dk_ref, dv_ref):
    h = _rms(x_ref[...], g_ref[...]).astype(wa_ref.dtype)
    d = jnp.dot(h, wb_ref[...], preferred_element_type=F32)
    gdn_ref[...] = jnp.dot(h, wa_ref[...], preferred_element_type=F32)
    ssd_ref[...] = jnp.dot(h, wc_ref[...], preferred_element_type=F32)
    ml_ref[...] = jnp.dot(h, wd_ref[...], preferred_element_type=F32)
    cos = cos_ref[...]
    sin = sin_ref[...]
    scale = DIFF_DH ** -0.5 * LOG2_E
    for half in range(2):
        lo = half * LANES
        dq_ref[:, lo:lo + LANES] = (_rope_half(d[:, lo:lo + LANES], cos, sin) * scale).astype(dq_ref.dtype)
        dk_ref[:, lo:lo + LANES] = _rope_half(d[:, MIX_W + lo:MIX_W + lo + LANES], cos, sin).astype(dk_ref.dtype)
    ones = jnp.ones((d.shape[0], HEAD_W), F32)
    for h in range(HEADS):
        v_h = d[:, 2 * MIX_W + h * HEAD_W:2 * MIX_W + (h + 1) * HEAD_W]
        dv_ref[0, h * LANES:(h + 1) * LANES, :] = jnp.concatenate([v_h, ones], axis=-1).T.astype(dv_ref.dtype)


def _pad_cols(w, width):
    return jnp.pad(w, ((0, 0), (0, width - w.shape[1])))


def _proj_call(x2, pre_g, w_in, cos_t, sin_t, seq, tm=FFN_TM):
    m, d = x2.shape
    wa = _pad_cols(w_in[:, _GDN_COLS[0]:_GDN_COLS[1]], SLAB_W).astype(MXU_DTYPE)
    wb = w_in[:, _DIFF_COLS[0]:_DIFF_COLS[1]].astype(MXU_DTYPE)
    wc = _pad_cols(w_in[:, _SSD_COLS[0]:_SSD_COLS[1]], SLAB_W).astype(MXU_DTYPE)
    wd = _pad_cols(w_in[:, _MLSTM_COLS[0]:_MLSTM_COLS[1]], SLAB_W).astype(MXU_DTYPE)
    tiles_per_seq = seq // tm
    full = lambda i: (0, 0)
    row = lambda i: (i, 0)
    return pl.pallas_call(
        _proj_kernel,
        out_shape=(
            jax.ShapeDtypeStruct((m, SLAB_W), F32),
            jax.ShapeDtypeStruct((m, SLAB_W), F32),
            jax.ShapeDtypeStruct((m, SLAB_W), F32),
            jax.ShapeDtypeStruct((m, MIX_W), MXU_DTYPE),
            jax.ShapeDtypeStruct((m, MIX_W), MXU_DTYPE),
            jax.ShapeDtypeStruct((m // tm, HEADS * LANES, tm), MXU_DTYPE),
        ),
        grid=(m // tm,),
        in_specs=[
            pl.BlockSpec((tm, d), row),
            pl.BlockSpec((1, d), full),
            pl.BlockSpec((d, SLAB_W), full),
            pl.BlockSpec((d, 3 * MIX_W), full),
            pl.BlockSpec((d, SLAB_W), full),
            pl.BlockSpec((d, SLAB_W), full),
            pl.BlockSpec((tm, LANES), lambda i: (i % tiles_per_seq, 0)),
            pl.BlockSpec((tm, LANES), lambda i: (i % tiles_per_seq, 0)),
        ],
        out_specs=(
            pl.BlockSpec((tm, SLAB_W), row),
            pl.BlockSpec((tm, SLAB_W), row),
            pl.BlockSpec((tm, SLAB_W), row),
            pl.BlockSpec((tm, MIX_W), row),
            pl.BlockSpec((tm, MIX_W), row),
            pl.BlockSpec((1, HEADS * LANES, tm), lambda i: (i, 0, 0)),
        ),
        compiler_params=pltpu.CompilerParams(dimension_semantics=("parallel",), vmem_limit_bytes=VMEM_LIMIT),
        name="mixer_in_proj",
    )(x2, pre_g.reshape(1, d), wa, wb, wc, wd, cos_t, sin_t)


def _attn_kernel(q_ref, k_ref, vt_ref, lam_ref, g_ref, o_ref, qt_ref, m_ref, acc_ref, p_ref, s_ref, alpha_ref,
                 *, tq, tk, nb, kt_per_seq, lambda_init):
    i = pl.program_id(1)
    seqs = range(nb)

    @pl.when(i == 0)
    def _():
        qt_ref[...] = jnp.zeros_like(qt_ref)

    for b in seqs:
        q_t = q_ref[b].astype(F32).T
        for mp in range(2 * HEADS):
            dims = slice(mp * DIFF_DH, (mp + 1) * DIFF_DH)
            qt_ref[b, mp // 2, dims, (mp % 2) * tq:(mp % 2 + 1) * tq] = q_t[dims, :].astype(qt_ref.dtype)
    m_ref[...] = jnp.full_like(m_ref, NEG)
    acc_ref[...] = jnp.zeros_like(acc_ref)
    qpos = i * tq + (_iota((1, 2 * tq), 1) & (tq - 1))
    limit = ((qpos >> CHUNK_SHIFT) + 1) << CHUNK_SHIFT
    n_kt = (i * tq + tq + tk - 1) // tk

    def scores(b, h, j, rows=tk):
        kt = k_ref[b, pl.ds(pl.multiple_of(j * tk, tk), rows), :]
        s_ref[b, h, 0:rows, :] = jnp.dot(kt, qt_ref[b, h], preferred_element_type=F32)

    def softmax(b, h, j, rows, mask_from):
        def piece(r0):
            sp = s_ref[b, h, r0:r0 + ATT_RB, :]
            if mask_from is not None and r0 >= mask_from:
                key = j * tk + r0 + _iota((ATT_RB, 1), 0)
                sp = jnp.where(key < limit, sp, NEG)
            return sp

        mx = piece(0)
        for r0 in range(ATT_RB, rows, ATT_RB):
            mx = jnp.maximum(mx, piece(r0))
        m_old = m_ref[b, h]
        m_new = jnp.maximum(m_old, jnp.max(mx, axis=0, keepdims=True))
        for r0 in range(0, rows, ATT_RB):
            p_ref[b, h, r0:r0 + ATT_RB, :] = jnp.exp2(piece(r0) - m_new).astype(p_ref.dtype)
        alpha_ref[b, h] = jnp.exp2(m_old - m_new)
        m_ref[b, h] = m_new

    def values(b, h, j, rows):
        pv = jnp.dot(vt_ref[b * kt_per_seq + j, h * LANES:(h + 1) * LANES, 0:rows], p_ref[b, h, 0:rows, :],
                     preferred_element_type=F32)
        acc_ref[b, h] = alpha_ref[b, h] * acc_ref[b, h] + pv

    def step(j, rows, mask_from, prefetch_scores):
        for h in range(HEADS):
            for b in seqs:
                softmax(b, h, j, rows, mask_from)
                if h + 2 < HEADS:
                    scores(b, h + 2, j, rows)
                elif prefetch_scores:
                    scores(b, h + 2 - HEADS, j + 1)
                values(b, h, j, rows)

    def body(j, carry):
        step(j, tk, None, True)
        return carry

    for h in range(2):
        for b in seqs:
            scores(b, h, 0)
    lax.fori_loop(0, n_kt - 1, body, 0)
    last = n_kt - 1
    diagonal_first = (i * tq) == last * tk

    @pl.when(diagonal_first)
    def _():
        step(last, tq, 0, False)

    @pl.when(jnp.logical_not(diagonal_first))
    def _():
        step(last, tk, tq, False)

    lam_p = lam_ref[...]
    lam = (jnp.exp(jnp.sum(lam_p[0:1] * lam_p[1:2], axis=-1, keepdims=True))
           - jnp.exp(jnp.sum(lam_p[2:3] * lam_p[3:4], axis=-1, keepdims=True)) + lambda_init)
    for b in seqs:
        outs = []
        for h in range(HEADS):
            a = acc_ref[b, h]
            o = (a[:HEAD_W, :tq] / a[HEAD_W:HEAD_W + 1, :tq]
                 - lam * (a[:HEAD_W, tq:] / a[HEAD_W:HEAD_W + 1, tq:]))
            outs.append(o * lax.rsqrt(jnp.mean(o * o, axis=0, keepdims=True) + EPS))
        o_ref[b] = jnp.concatenate(outs, axis=0).T * g_ref[...] * (1.0 - lambda_init)


def _attn_call(dq, dk, dvt, lam_p, norm_g, lambda_init, bsz, seq, tq=ATT_TQ, tk=ATT_TK, nb=ATT_NB):
    assert tk == 2 * tq, "the last-key-tile split in _attn_kernel assumes tk == 2 tq"
    q3 = dq.reshape(bsz, seq, MIX_W)
    k3 = dk.reshape(bsz, seq, MIX_W)
    n_kt = seq // tk
    out = pl.pallas_call(
        functools.partial(_attn_kernel, tq=tq, tk=tk, nb=nb, kt_per_seq=n_kt, lambda_init=lambda_init),
        out_shape=jax.ShapeDtypeStruct((bsz, seq, MIX_W), F32),
        grid=(bsz // nb, seq // tq),
        in_specs=[
            pl.BlockSpec((nb, tq, MIX_W), lambda b, i: (b, i, 0)),
            pl.BlockSpec((nb, seq, MIX_W), lambda b, i: (b, 0, 0)),
            pl.BlockSpec((nb * n_kt, HEADS * LANES, tk), lambda b, i: (b, 0, 0)),
            pl.BlockSpec((4, DIFF_DH), lambda b, i: (0, 0)),
            pl.BlockSpec((1, MIX_W), lambda b, i: (0, 0)),
        ],
        out_specs=pl.BlockSpec((nb, tq, MIX_W), lambda b, i: (b, i, 0)),
        scratch_shapes=[
            pltpu.VMEM((nb, HEADS, MIX_W, 2 * tq), MXU_DTYPE),
            pltpu.VMEM((nb, HEADS, 1, 2 * tq), F32),
            pltpu.VMEM((nb, HEADS, LANES, 2 * tq), F32),
            pltpu.VMEM((nb, HEADS, tk, 2 * tq), MXU_DTYPE),
            pltpu.VMEM((nb, HEADS, tk, 2 * tq), F32),
            pltpu.VMEM((nb, HEADS, 1, 2 * tq), F32),
        ],
        compiler_params=pltpu.CompilerParams(dimension_semantics=("parallel", "arbitrary"),
                                             vmem_limit_bytes=VMEM_LIMIT),
        name="diff_attention",
    )(q3, k3, dvt, lam_p, jnp.tile(norm_g, HEADS).reshape(1, MIX_W))
    return out.reshape(bsz * seq, MIX_W)


def _head_block(x, h):
    blk = x[:, (h // 2) * LANES:(h // 2 + 1) * LANES]
    own = (_iota(blk.shape, 1) >> CHUNK_SHIFT) == (h % 2)
    return jnp.where(own, blk, jnp.zeros_like(blk))


def _bd_rows(blocks):
    zero = jnp.zeros_like(blocks[0])
    return jnp.concatenate([jnp.concatenate([b, zero] if h < 2 else [zero, b], axis=1)
                            for h, b in enumerate(blocks)], axis=0)


def _bd(x):
    return _bd_rows([_head_block(x, h) for h in range(HEADS)])


def _bd_keep(m):
    return _bd_rows([_head_block(m[h * CHUNK:(h + 1) * CHUNK, :], h) for h in range(HEADS)])


def _split(x):
    hi = x.astype(MXU_DTYPE)
    return hi, (x - hi.astype(F32)).astype(MXU_DTYPE)


def _bd_matmul3(a, b):
    ah, al = _split(a)
    bh, bl = _split(b)
    m = a.shape[0]
    r = jnp.dot(jnp.concatenate([ah, al], axis=0), _bd(bh), preferred_element_type=F32)
    return r[:m] + r[m:] + jnp.dot(ah, _bd(bl), preferred_element_type=F32)


def _chunk_slices(t):
    return [slice(c * CHUNK, (c + 1) * CHUNK) for c in range(t // CHUNK)]


def _ssd_steps(slab_ref, cw_ref, cb_ref, alog_ref, dtb_ref, dskip_ref, g_ref, y_ref,
               carry_ref, buf_ref, st_ref, xc_ref, xcd_ref, bm_ref, cm_ref, lm_ref, ea_ref, el_ref, yo_ref, t, b):
    ltile, _, pos, mm = _chunk_consts(t)
    expand = _head_expand(0)
    xbc = _silu(_causal_conv(slab_ref[b,:, MIX_W:4 * MIX_W], cw_ref, carry_ref, buf_ref, t) + cb_ref[...])
    x = xbc[:, 0:MIX_W]
    bm_ref[...] = xbc[:, MIX_W:2 * MIX_W].astype(bm_ref.dtype)
    cm_ref[...] = xbc[:, 2 * MIX_W:3 * MIX_W].astype(cm_ref.dtype)
    yield
    dt = _softplus(slab_ref[b,:, 4 * MIX_W:] + dtb_ref[...])
    da = dt * (-jnp.exp(alog_ref[...]))
    dt_e = _dot_xsel(dt, expand)
    da_e = _dot_xsel(da, expand)
    yield
    acs_e = _dot_sel(ltile, da_e)
    acs_last_e = _chunk_last(acs_e)
    ldiff = _dot_sel(ltile, jnp.where(pos > mm, da_e, 0.0))
    lm_ref[...] = jnp.where(mm <= pos, jnp.exp(ldiff), 0.0)
    xc = x * dt_e
    xc_ref[...] = xc.astype(xc_ref.dtype)
    xcd_ref[...] = (xc * jnp.exp(acs_last_e - acs_e)).astype(xcd_ref.dtype)
    ea_ref[...] = jnp.exp(acs_e)
    el_ref[...] = jnp.exp(acs_last_e)
    yield
    for c in range(t // CHUNK):
        r0 = c * CHUNK
        rows = slice(r0, r0 + CHUNK)
        for grp in range(2):
            gl = slice(grp * SSD_N, (grp + 1) * SSD_N)
            bg = bm_ref[rows, gl]
            cg = cm_ref[rows, gl]
            cb = _dot_nt(cg, bg)
            ys = []
            for hh in range(2):
                hl = slice(grp * SSD_N + hh * HEAD_W, grp * SSD_N + (hh + 1) * HEAD_W)
                ys.append(_dot(cb * lm_ref[rows, hl], xc_ref[rows, hl]))
            y_diag = jnp.concatenate(ys, axis=-1)
            st = st_ref[grp]
            y_off = _dot(cg, st) * ea_ref[rows, gl]
            st_ref[grp] = st * el_ref[r0:r0 + 1, gl] + _dot_tn(bg, xcd_ref[rows, gl])
            yo_ref[rows, gl] = y_diag + y_off
        yield
    y = (yo_ref[...] + x * dskip_ref[...]) * _silu(slab_ref[b,:, 0:MIX_W])
    gs = g_ref[...]
    for grp in range(2):
        gl = slice(grp * SSD_N, (grp + 1) * SSD_N)
        y_ref[b, :, gl] = _rms(y[:, gl], gs[:, gl])


def _gdn_steps(slab_ref, cw_ref, alog_ref, dtb_ref, g_ref, y_ref,
               carry_ref, buf_ref, st_ref, qn_ref, kn_ref, qd_ref, kd_ref, kb_ref, vb_ref, kbg_ref, dec_ref,
               egl_ref, u_ref, w_ref, attn_ref, o_ref, t, b):
    ltile, _, pos, mm = _chunk_consts(t)
    ones_h = _head_ones()
    qkv = _silu(_causal_conv(slab_ref[b,:, 0:3 * MIX_W], cw_ref, carry_ref, buf_ref, t))
    q = qkv[:, 0:MIX_W]
    k = qkv[:, MIX_W:2 * MIX_W]
    v = qkv[:, 2 * MIX_W:]
    yield
    qn = q * lax.rsqrt(_dot_xsel(q * q, ones_h, terms=2) + EPS) * (HEAD_W ** -0.5)
    kn = k * lax.rsqrt(_dot_xsel(k * k, ones_h, terms=2) + EPS)
    yield
    ba = slab_ref[b,:, 4 * MIX_W:]
    beta_e = _dot_xsel(_sigmoid(ba), _head_expand(0))
    g4 = -jnp.exp(alog_ref[...]) * _softplus(ba + dtb_ref[...])
    g_e = _dot_xsel(g4, _head_expand(HEADS))
    yield
    gc_e = _dot_sel(ltile, g_e)
    gl_e = _chunk_last(gc_e)
    diff = _dot_sel(ltile, jnp.where(pos > mm, g_e, 0.0))
    dec_ref[...] = jnp.where(mm <= pos, jnp.exp(diff), 0.0)
    egc = jnp.exp(gc_e)
    kb = kn * beta_e
    qn_ref[...] = qn
    kn_ref[...] = kn
    qd_ref[...] = qn * egc
    kd_ref[...] = kn * jnp.exp(gl_e - gc_e)
    kb_ref[...] = kb
    vb_ref[...] = v * beta_e
    kbg_ref[...] = kb * egc
    egl_ref[...] = jnp.exp(gl_e)
    yield

    r_ss = _iota((CHUNK, MIX_W), 0)
    m_ss = _iota((CHUNK, MIX_W), 1) & (CHUNK - 1)
    eye_ss = jnp.where(r_ss == m_ss, 1.0, 0.0).astype(F32)
    chunk_rows = _chunk_slices(t)

    ys, tms = [], []
    for rows in chunk_rows:
        lhs = jnp.concatenate([kb_ref[rows, :], qn_ref[rows, :]], axis=0).astype(MXU_DTYPE)
        raw = lax.dot_general(lhs, _bd(kn_ref[rows, :].astype(MXU_DTYPE)), (((1,), (1,)), ((), ())),
                              preferred_element_type=F32)
        dec = dec_ref[rows, :]
        attn_ref[rows, :] = raw[CHUNK:] * dec
        y = jnp.where(m_ss < r_ss, -(raw[:CHUNK] * dec), 0.0)
        ys.append(y)
        tms.append(eye_ss + y)
    yield
    ys = [_bd_matmul3(y, y) for y in ys]
    yield
    for _ in range(4):
        rs = [_bd_matmul3(jnp.concatenate([tm, y], axis=0), y) for tm, y in zip(tms, ys)]
        tms = [tm + r[:CHUNK] for tm, r in zip(tms, rs)]
        ys = [r[CHUNK:] for r in rs]
        yield
    tms = [tm + _bd_matmul3(tm, y) for tm, y in zip(tms, ys)]
    yield
    for rows, tm in zip(chunk_rows, tms):
        rhs = jnp.concatenate([_bd(vb_ref[rows, :].astype(MXU_DTYPE)),
                               _bd(kbg_ref[rows, :].astype(MXU_DTYPE))], axis=1)
        uw = jnp.dot(tm.astype(MXU_DTYPE), rhs, preferred_element_type=F32)
        u_ref[rows, :] = uw[:, :MIX_W]
        w_ref[rows, :] = uw[:, MIX_W:]
    yield

    st = st_ref[...]
    for c, rows in enumerate(chunk_rows):
        lhs = jnp.concatenate([w_ref[rows, :], qd_ref[rows, :]], axis=0).astype(MXU_DTYPE)
        r = jnp.dot(lhs, st.astype(MXU_DTYPE), preferred_element_type=F32)
        v_new = u_ref[rows, :] - r[:CHUNK]
        v_new_c = v_new.astype(MXU_DTYPE)
        o_ref[rows, :] = r[CHUNK:] + jnp.dot(attn_ref[rows, :].astype(MXU_DTYPE), _bd(v_new_c),
                                             preferred_element_type=F32)
        kv = lax.dot_general(kd_ref[rows, :].astype(MXU_DTYPE), v_new_c, (((0,), (0,)), ((), ())),
                             preferred_element_type=F32)
        st = st * egl_ref[c * CHUNK:c * CHUNK + 1, :] + _bd_keep(kv)
        yield
    st_ref[...] = st

    o = o_ref[...]
    o = o * lax.rsqrt(_dot_xsel(o * o, ones_h, terms=2) * (1.0 / HEAD_W) + EPS) * g_ref[...]
    y_ref[b] =o * _silu(slab_ref[b,:, 3 * MIX_W:4 * MIX_W])


def _mlstm_steps(slab_ref, bias_ref, g_ref, y_ref, st_ref, m_ref, dl_ref, b_ref, ge_ref, dmax_ref, qk_ref,
                 o_ref, t, b):
    ltile, csame, pos, mm = _chunk_consts(t)
    gates = slab_ref[b,:, 4 * MIX_W:] + bias_ref[...]
    li_e = _dot_xsel(gates, _head_expand(0))
    lf_e = _dot_xsel(-_softplus(-gates), _head_expand(HEADS))
    yield
    b_e = _dot_sel(ltile, lf_e)
    dlog = _dot_sel(ltile, jnp.where(pos > mm, lf_e, 0.0)) + _dot_sel(csame, jnp.where(pos == mm, li_e, 0.0))
    dl = jnp.where(mm <= pos, dlog, NEG)
    yield
    lane_head = _iota((t, MIX_W), 1) >> CHUNK_SHIFT
    dmax = jnp.full((t, MIX_W), NEG, F32)
    for h in range(HEADS):
        in_head = lane_head == h
        dmax = jnp.where(in_head, jnp.max(jnp.where(in_head, dl, NEG), axis=-1, keepdims=True), dmax)
    dl_ref[...] = dl
    dmax_ref[...] = dmax
    b_ref[...] = b_e
    ge_ref[...] = _chunk_last(b_e) - b_e + li_e
    yield

    ones_bd = _head_ones().astype(MXU_DTYPE)
    ones_rows = jnp.ones((CHUNK, MIX_W), MXU_DTYPE)
    chunk_rows = _chunk_slices(t)
    k_scale = HEAD_W ** -0.5
    for rows in chunk_rows:
        k_c = (slab_ref[b,rows, MIX_W:2 * MIX_W] * k_scale).astype(MXU_DTYPE)
        qk_ref[rows, :] = lax.dot_general(slab_ref[b,rows, 0:MIX_W].astype(MXU_DTYPE), _bd(k_c),
                                          (((1,), (1,)), ((), ())), preferred_element_type=F32)
    yield

    st = st_ref[...]
    m_st = m_ref[...]
    for rows in chunk_rows:
        q_c = slab_ref[b,rows, 0:MIX_W].astype(MXU_DTYPE)
        k_c = slab_ref[b,rows, MIX_W:2 * MIX_W] * k_scale
        v_c = slab_ref[b,rows, 2 * MIX_W:3 * MIX_W].astype(MXU_DTYPE)
        b_c = b_ref[rows, :]
        m_t = jnp.maximum(b_c + m_st, dmax_ref[rows, :])
        w_st = jnp.exp(b_c + m_st - m_t)
        s = qk_ref[rows, :] * jnp.exp(dl_ref[rows, :] - m_t)
        qs = jnp.dot(q_c, st.astype(MXU_DTYPE), preferred_element_type=F32)
        sv = jnp.dot(s.astype(MXU_DTYPE), jnp.concatenate([_bd(v_c), ones_bd], axis=1),
                     preferred_element_type=F32)
        num = w_st * qs[:, :MIX_W] + sv[:, :MIX_W]
        den = w_st * qs[:, MIX_W:] + sv[:, MIX_W:]
        o_ref[rows, :] = num / jnp.maximum(jnp.abs(den), jnp.exp(-m_t))
        ge_c = ge_ref[rows, :]
        b_last = b_c[CHUNK - 1:CHUNK, :]
        m_new = jnp.maximum(b_last + m_st, jnp.max(ge_c, axis=0, keepdims=True))
        w_old = jnp.exp(b_last + m_st - m_new)
        k_w = (k_c * jnp.exp(ge_c - m_new)).astype(MXU_DTYPE)
        upd = lax.dot_general(k_w, jnp.concatenate([v_c, ones_rows], axis=1), (((0,), (0,)), ((), ())),
                              preferred_element_type=F32)
        st = (st * jnp.concatenate([w_old, w_old], axis=1)
              + jnp.concatenate([_bd_keep(upd[:, :MIX_W]), _bd_keep(upd[:, MIX_W:])], axis=1))
        m_st = m_new
        yield
    st_ref[...] = st
    m_ref[...] = m_st

    hcat = o_ref[...] * _sigmoid(slab_ref[b,:, 3 * MIX_W:4 * MIX_W])
    ms = _dot_xsel(hcat * hcat, _head_ones(), terms=2) * (1.0 / HEAD_W)
    y_ref[b] =hcat * lax.rsqrt(ms + EPS) * g_ref[...]


_N_GDN_REFS = (5, 1, 16)
_N_SSD_REFS = (7, 1, 11)
_N_MLSTM_REFS = (3, 1, 8)


def _mixers_kernel(*refs, t, nb):
    counts = (_N_GDN_REFS, _N_SSD_REFS, _N_MLSTM_REFS)
    pos = 0
    ins, outs = [], []
    for c in counts:
        ins.append(refs[pos:pos + c[0]])
        pos += c[0]
    for c in counts:
        outs.append(refs[pos:pos + c[1]])
        pos += c[1]
    scratch = []
    for b in range(nb):
        per_mixer = []
        for c in counts:
            per_mixer.append(refs[pos:pos + c[2]])
            pos += c[2]
        scratch.append(per_mixer)

    @pl.when(pl.program_id(1) == 0)
    def _():
        for b in range(nb):
            gdn_s, ssd_s, ml_s = scratch[b]
            for state_ref in (gdn_s[0], gdn_s[2], ssd_s[0], ssd_s[2], ml_s[0], ml_s[1]):
                state_ref[...] = jnp.zeros_like(state_ref)

    gdn = [_gdn_steps(*ins[0], *outs[0], *scratch[b][0], t, b) for b in range(nb)]
    ssd = [_ssd_steps(*ins[1], *outs[1], *scratch[b][1], t, b) for b in range(nb)]
    mls = [_mlstm_steps(*ins[2], *outs[2], *scratch[b][2], t, b) for b in range(nb)]
    streams = gdn + gdn + ssd + mls
    while streams:
        for stream in list(streams):
            if stream in streams and next(stream, StopIteration) is StopIteration:
                streams = [s for s in streams if s is not stream]


def _mixers_call(gdn_slab, ssd_slab, ml_slab, p, bsz, seq, t=MIX_T, nb=MIX_NB):
    ch = 3 * MIX_W
    lane_pad = lambda v, before: jnp.pad(v, (before, LANES - before - v.shape[0])).reshape(1, LANES)
    full = lambda b, i: (0, 0)
    slab_spec = pl.BlockSpec((nb, t, SLAB_W), lambda b, i: (b, i, 0))
    vec = lambda n: pl.BlockSpec((1, n), full)
    tile = lambda dtype=F32: pltpu.VMEM((t, MIX_W), dtype)
    out_spec = pl.BlockSpec((nb, t, MIX_W), lambda b, i: (b, i, 0))
    out_shape = jax.ShapeDtypeStruct((bsz, seq, MIX_W), F32)
    conv_scratch = lambda: [pltpu.VMEM((8, ch), F32), pltpu.VMEM((t + 8, ch), F32)]
    gdn_in = [gdn_slab.reshape(bsz, seq, SLAB_W), p["gdn_conv_w"], lane_pad(p["gdn_a_log"], HEADS),
              lane_pad(p["gdn_dt_bias"], HEADS), jnp.tile(p["gdn_norm_g"], HEADS).reshape(1, MIX_W)]
    gdn_specs = [slab_spec, pl.BlockSpec((CONV_K, ch), full), vec(LANES), vec(LANES), vec(MIX_W)]
    gdn_scratch = lambda: conv_scratch() + [pltpu.VMEM((MIX_W, MIX_W), F32)] + [tile() for _ in range(13)]
    ssd_in = [ssd_slab.reshape(bsz, seq, SLAB_W), p["ssd_conv_w"], p["ssd_conv_b"].reshape(1, ch),
              lane_pad(p["ssd_a_log"], 0), lane_pad(p["ssd_dt_bias"], 0),
              jnp.repeat(p["ssd_d"], HEAD_W).reshape(1, MIX_W), p["ssd_norm_g"].reshape(1, MIX_W)]
    ssd_specs = [slab_spec, pl.BlockSpec((CONV_K, ch), full), vec(ch), vec(LANES), vec(LANES), vec(MIX_W), vec(MIX_W)]
    ssd_scratch = lambda: (conv_scratch() + [pltpu.VMEM((2, SSD_N, SSD_N), F32)]
                           + [tile(MXU_DTYPE) for _ in range(4)] + [tile() for _ in range(4)])
    ml_in = [ml_slab.reshape(bsz, seq, SLAB_W),
             lane_pad(jnp.concatenate([p["mlstm_i_bias"], p["mlstm_f_bias"]]), 0), p["mlstm_norm_g"].reshape(1, MIX_W)]
    ml_specs = [slab_spec, vec(LANES), vec(MIX_W)]
    ml_scratch = lambda: ([pltpu.VMEM((MIX_W, 2 * MIX_W), F32), pltpu.VMEM((1, MIX_W), F32)]
                          + [tile() for _ in range(6)])
    assert (len(gdn_in), 1, len(gdn_scratch())) == _N_GDN_REFS
    assert (len(ssd_in), 1, len(ssd_scratch())) == _N_SSD_REFS
    assert (len(ml_in), 1, len(ml_scratch())) == _N_MLSTM_REFS
    scratch = []
    for _ in range(nb):
        scratch += gdn_scratch() + ssd_scratch() + ml_scratch()
    outs = pl.pallas_call(
        functools.partial(_mixers_kernel, t=t, nb=nb),
        out_shape=(out_shape, out_shape, out_shape),
        grid=(bsz // nb, seq // t),
        in_specs=gdn_specs + ssd_specs + ml_specs,
        out_specs=(out_spec, out_spec, out_spec),
        scratch_shapes=scratch,
        compiler_params=pltpu.CompilerParams(dimension_semantics=("parallel", "arbitrary"),
                                             vmem_limit_bytes=VMEM_LIMIT),
        name="recurrent_mixers",
    )(*gdn_in, *ssd_in, *ml_in)
    return [o.reshape(bsz * seq, MIX_W) for o in outs]


def _rope_tables(seq):
    inv_freq = ROPE_THETA ** (-jnp.arange(0, DIFF_DH, 2, dtype=F32) / DIFF_DH)
    ang = jnp.arange(seq, dtype=F32)[:, None] * inv_freq[None, :]
    cos, sin = jnp.cos(ang), jnp.sin(ang)
    reps = LANES // DIFF_DH
    cos_t = jnp.tile(jnp.concatenate([cos, cos], axis=-1), (1, reps))
    sin_t = jnp.tile(jnp.concatenate([-sin, sin], axis=-1), (1, reps))
    return cos_t, sin_t


def kernel(x, ffn1_pre_g, ffn1_w_gate, ffn1_w_up, ffn1_w_down, ffn1_post_g, mix_pre_g, w_in, gdn_conv_w, gdn_a_log, gdn_dt_bias, gdn_norm_g, diff_lam_q1, diff_lam_k1, diff_lam_q2, diff_lam_k2, diff_norm_g, ssd_conv_w, ssd_conv_b, ssd_a_log, ssd_dt_bias, ssd_d, ssd_norm_g, mlstm_i_bias, mlstm_f_bias, mlstm_norm_g, w_out, mix_post_g, ffn2_pre_g, ffn2_w_gate, ffn2_w_up, ffn2_w_down, ffn2_post_g):
    bsz, seq, d = x.shape
    x2 = x.reshape(bsz * seq, d)
    cos_t, sin_t = _rope_tables(seq)
    for l in range(DEPTH):
        lambda_init = 0.8 - 0.6 * math.exp(-0.3 * l)
        x2 = _ffn_call(x2, ffn1_pre_g[l], ffn1_w_gate[l], ffn1_w_up[l], ffn1_w_down[l], ffn1_post_g[l])
        gdn_slab, ssd_slab, ml_slab, dq, dk, dv = _proj_call(x2, mix_pre_g[l], w_in[l], cos_t, sin_t, seq)
        mixer_params = dict(
            gdn_conv_w=gdn_conv_w[l], gdn_a_log=gdn_a_log[l], gdn_dt_bias=gdn_dt_bias[l], gdn_norm_g=gdn_norm_g[l],
            ssd_conv_w=ssd_conv_w[l], ssd_conv_b=ssd_conv_b[l], ssd_a_log=ssd_a_log[l], ssd_dt_bias=ssd_dt_bias[l],
            ssd_d=ssd_d[l], ssd_norm_g=ssd_norm_g[l],
            mlstm_i_bias=mlstm_i_bias[l], mlstm_f_bias=mlstm_f_bias[l], mlstm_norm_g=mlstm_norm_g[l])
        out_a, out_c, out_d = _mixers_call(gdn_slab, ssd_slab, ml_slab, mixer_params, bsz, seq)
        lam_p = jnp.stack([diff_lam_q1[l], diff_lam_k1[l], diff_lam_q2[l], diff_lam_k2[l]])
        out_b = _attn_call(dq, dk, dv, lam_p, diff_norm_g[l], lambda_init, bsz, seq)
        x2 = _ffn_call(x2, ffn2_pre_g[l], ffn2_w_gate[l], ffn2_w_up[l], ffn2_w_down[l], ffn2_post_g[l],
                       mix=((out_a, out_b, out_c, out_d), w_out[l], mix_post_g[l]))
    return x2.reshape(bsz, seq, d)
```

```python
import functools
import math

import jax
import jax.numpy as jnp
from jax import lax
from jax.experimental import pallas as pl
from jax.experimental.pallas import tpu as pltpu

F32 = jnp.float32
MXU_DTYPE = jnp.bfloat16

D_MODEL = 1024
D_FF = 2816
DEPTH = 2
CHUNK = 64
CHUNK_SHIFT = 6
CONV_K = 4
EPS = 1e-6
ROPE_THETA = 10000.0
HEADS = 4
HEAD_W = 64
MIX_W = HEADS * HEAD_W
DIFF_DH = 32
DIFF_DH_SHIFT = 5
LOG2_E = math.log2(math.e)
SSD_N = 128
LANES = 128
NEG = -1e30

_GDN_COLS = (0, 1032)
_DIFF_COLS = (1032, 1800)
_SSD_COLS = (1800, 2828)
_MLSTM_COLS = (2828, 3860)
SLAB_W = 1152

FFN_TM = 512
FFN_FC = 256
MIX_T = 256
MIX_NB = 2
ATT_TQ = 256
ATT_NB = 2
ATT_RB = 16
ATT_TK = FFN_TM
VMEM_LIMIT = 56 * 1024 * 1024


def _dot(a, b):
    return jnp.dot(a.astype(MXU_DTYPE), b.astype(MXU_DTYPE), preferred_element_type=F32)


def _dot_nt(a, b):
    return lax.dot_general(a.astype(MXU_DTYPE), b.astype(MXU_DTYPE), (((1,), (1,)), ((), ())),
                           preferred_element_type=F32)


def _dot_tn(a, b):
    return lax.dot_general(a.astype(MXU_DTYPE), b.astype(MXU_DTYPE), (((0,), (0,)), ((), ())),
                           preferred_element_type=F32)


def _split3(x):
    x1 = x.astype(MXU_DTYPE)
    r1 = x - x1.astype(F32)
    x2 = r1.astype(MXU_DTYPE)
    return x1, x2, (r1 - x2.astype(F32)).astype(MXU_DTYPE)


def _dot_sel(sel, x, terms=3):
    sel = sel.astype(MXU_DTYPE)
    return sum(jnp.dot(sel, xi, preferred_element_type=F32) for xi in _split3(x)[:terms])


def _dot_xsel(x, sel, terms=3):
    sel = sel.astype(MXU_DTYPE)
    return sum(jnp.dot(xi, sel, preferred_element_type=F32) for xi in _split3(x)[:terms])


def _chunk_last(x):
    t, w = x.shape
    return jnp.concatenate([jnp.broadcast_to(x[c + CHUNK - 1:c + CHUNK, :], (CHUNK, w))
                            for c in range(0, t, CHUNK)], axis=0)


def _rms(x, g):
    return x * lax.rsqrt(jnp.mean(x * x, axis=-1, keepdims=True) + EPS) * g


def _silu(x):
    return x / (1.0 + jnp.exp(-x))


def _sigmoid(x):
    return 1.0 / (1.0 + jnp.exp(-x))


def _softplus(x):
    return jnp.maximum(x, 0.0) + jnp.log(1.0 + jnp.exp(-jnp.abs(x)))


def _iota(shape, dim):
    return lax.broadcasted_iota(jnp.int32, shape, dim)


def _chunk_consts(t):
    r = _iota((t, t), 0)
    c = _iota((t, t), 1)
    same = (r >> CHUNK_SHIFT) == (c >> CHUNK_SHIFT)
    ltile = jnp.where(same & (c <= r), 1.0, 0.0).astype(F32)
    csame = jnp.where(same, 1.0, 0.0).astype(F32)
    pos = _iota((t, MIX_W), 0) & (CHUNK - 1)
    m = _iota((t, MIX_W), 1) & (CHUNK - 1)
    return ltile, csame, pos, m


def _head_expand(first_lane):
    r = _iota((LANES, MIX_W), 0)
    c = _iota((LANES, MIX_W), 1)
    return jnp.where(r == (c >> CHUNK_SHIFT) + first_lane, 1.0, 0.0).astype(F32)


def _head_mask():
    return (_iota((MIX_W, MIX_W), 0) >> CHUNK_SHIFT) == (_iota((MIX_W, MIX_W), 1) >> CHUNK_SHIFT)


def _head_ones():
    return jnp.where(_head_mask(), 1.0, 0.0).astype(F32)


def _causal_conv(u, w_ref, carry_ref, buf_ref, t):
    buf_ref[0:8, :] = carry_ref[...]
    buf_ref[8:8 + t, :] = u
    carry_ref[...] = u[t - 8:t, :]
    acc = w_ref[CONV_K - 1:CONV_K, :] * u
    for j in range(CONV_K - 1):
        off = 8 - (CONV_K - 1) + j
        acc = acc + w_ref[j:j + 1, :] * buf_ref[off:off + t, :]
    return acc


def _ffn_body(x, pre_ref, wg_ref, wu_ref, wd_ref, post_ref, h_ref, acc_ref, fc):
    h_ref[...] = _rms(x, pre_ref[...]).astype(h_ref.dtype)
    acc_ref[...] = jnp.zeros_like(acc_ref)
    for c0 in range(0, wg_ref.shape[1], fc):
        g = jnp.dot(h_ref[...], wg_ref[:, c0:c0 + fc], preferred_element_type=F32)
        u = jnp.dot(h_ref[...], wu_ref[:, c0:c0 + fc], preferred_element_type=F32)
        a = (_silu(g) * u).astype(wd_ref.dtype)
        acc_ref[...] += jnp.dot(a, wd_ref[c0:c0 + fc, :], preferred_element_type=F32)
    return x + 0.5 * _rms(acc_ref[...], post_ref[...])


def _ffn_kernel(x_ref, pre_ref, wg_ref, wu_ref, wd_ref, post_ref, o_ref, h_ref, acc_ref, *, fc):
    o_ref[...] = _ffn_body(x_ref[...], pre_ref, wg_ref, wu_ref, wd_ref, post_ref, h_ref, acc_ref, fc)


def _outproj_ffn_kernel(x_ref, a_ref, b_ref, c_ref, d_ref, wo_ref, go_ref, pre_ref, wg_ref, wu_ref, wd_ref, post_ref,
                        o_ref, h_ref, acc_ref, *, fc):
    mixed = jnp.concatenate([a_ref[...], b_ref[...], c_ref[...], d_ref[...]], axis=-1).astype(wo_ref.dtype)
    x = x_ref[...] + _rms(jnp.dot(mixed, wo_ref[...], preferred_element_type=F32), go_ref[...])
    o_ref[...] = _ffn_body(x, pre_ref, wg_ref, wu_ref, wd_ref, post_ref, h_ref, acc_ref, fc)


def _resident(shape):
    return pl.BlockSpec(shape, lambda i: (0,) * len(shape), pipeline_mode=pl.Buffered(1))


def _ffn_call(x2, pre_g, w_gate, w_up, w_down, post_g, mix=None, tm=FFN_TM, fc=FFN_FC):
    m, d = x2.shape
    ff = w_gate.shape[1]
    row = lambda i: (i, 0)
    ffn_specs = [_resident((1, d)), _resident((d, ff)), _resident((d, ff)), _resident((ff, d)), _resident((1, d))]
    ffn_args = (pre_g.reshape(1, d), w_gate.astype(MXU_DTYPE), w_up.astype(MXU_DTYPE), w_down.astype(MXU_DTYPE),
                post_g.reshape(1, d))
    if mix is None:
        body, name = _ffn_kernel, "ffn"
        specs, args = ffn_specs, ffn_args
    else:
        outs, w_out, mix_post_g = mix
        body, name = _outproj_ffn_kernel, "mixer_out_proj_ffn"
        specs = [pl.BlockSpec((tm, MIX_W), row)] * 4 + [_resident((d, d)), _resident((1, d))] + ffn_specs
        args = (*outs, w_out.astype(MXU_DTYPE), mix_post_g.reshape(1, d)) + ffn_args
    return pl.pallas_call(
        functools.partial(body, fc=fc),
        out_shape=jax.ShapeDtypeStruct((m, d), F32),
        grid=(m // tm,),
        in_specs=[pl.BlockSpec((tm, d), row)] + specs,
        out_specs=pl.BlockSpec((tm, d), row),
        scratch_shapes=[pltpu.VMEM((tm, d), MXU_DTYPE), pltpu.VMEM((tm, d), F32)],
        compiler_params=pltpu.CompilerParams(dimension_semantics=("parallel",), vmem_limit_bytes=VMEM_LIMIT),
        name=name,
    )(x2, *args)


def _rope_half(x, cos, sin):
    lane = _iota(x.shape, 1)
    half = DIFF_DH // 2
    partner = jnp.where((lane & (DIFF_DH - 1)) < half, pltpu.roll(x, LANES - half, 1), pltpu.roll(x, half, 1))
    return x * cos + partner * sin


def _proj_kernel(x_ref, g_ref, wa_ref, wb_ref, wc_ref, wd_ref, cos_ref, sin_ref,
                 gdn_ref, ssd_ref, ml_ref, dq_ref, dk_ref, dv_ref):
    h = _rms(x_ref[...], g_ref[...]).astype(wa_ref.dtype)
    d = jnp.dot(h, wb_ref[...], preferred_element_type=F32)
    gdn_ref[...] = jnp.dot(h, wa_ref[...], preferred_element_type=F32)
    ssd_ref[...] = jnp.dot(h, wc_ref[...], preferred_element_type=F32)
    ml_ref[...] = jnp.dot(h, wd_ref[...], preferred_element_type=F32)
    cos = cos_ref[...]
    sin = sin_ref[...]
    scale = DIFF_DH ** -0.5 * LOG2_E
    for half in range(2):
        lo = half * LANES
        dq_ref[:, lo:lo + LANES] = (_rope_half(d[:, lo:lo + LANES], cos, sin) * scale).astype(dq_ref.dtype)
        dk_ref[:, lo:lo + LANES] = _rope_half(d[:, MIX_W + lo:MIX_W + lo + LANES], cos, sin).astype(dk_ref.dtype)
    ones = jnp.ones((d.shape[0], HEAD_W), F32)
    for h in range(HEADS):
        v_h = d[:, 2 * MIX_W + h * HEAD_W:2 * MIX_W + (h + 1) * HEAD_W]
        dv_ref[0, h * LANES:(h + 1) * LANES, :] = jnp.concatenate([v_h, ones], axis=-1).T.astype(dv_ref.dtype)


def _pad_cols(w, width):
    return jnp.pad(w, ((0, 0), (0, width - w.shape[1])))


def _proj_call(x2, pre_g, w_in, cos_t, sin_t, seq, tm=FFN_TM):
    m, d = x2.shape
    wa = _pad_cols(w_in[:, _GDN_COLS[0]:_GDN_COLS[1]], SLAB_W).astype(MXU_DTYPE)
    wb = w_in[:, _DIFF_COLS[0]:_DIFF_COLS[1]].astype(MXU_DTYPE)
    wc = _pad_cols(w_in[:, _SSD_COLS[0]:_SSD_COLS[1]], SLAB_W).astype(MXU_DTYPE)
    wd = _pad_cols(w_in[:, _MLSTM_COLS[0]:_MLSTM_COLS[1]], SLAB_W).astype(MXU_DTYPE)
    tiles_per_seq = seq // tm
    full = lambda i: (0, 0)
    row = lambda i: (i, 0)
    return pl.pallas_call(
        _proj_kernel,
        out_shape=(
            jax.ShapeDtypeStruct((m, SLAB_W), F32),
            jax.ShapeDtypeStruct((m, SLAB_W), F32),
            jax.ShapeDtypeStruct((m, SLAB_W), F32),
            jax.ShapeDtypeStruct((m, MIX_W), MXU_DTYPE),
            jax.ShapeDtypeStruct((m, MIX_W), MXU_DTYPE),
            jax.ShapeDtypeStruct((m // tm, HEADS * LANES, tm), MXU_DTYPE),
        ),
        grid=(m // tm,),
        in_specs=[
            pl.BlockSpec((tm, d), row),
            pl.BlockSpec((1, d), full),
            pl.BlockSpec((d, SLAB_W), full),
            pl.BlockSpec((d, 3 * MIX_W), full),
            pl.BlockSpec((d, SLAB_W), full),
            pl.BlockSpec((d, SLAB_W), full),
            pl.BlockSpec((tm, LANES), lambda i: (i % tiles_per_seq, 0)),
            pl.BlockSpec((tm, LANES), lambda i: (i % tiles_per_seq, 0)),
        ],
        out_specs=(
            pl.BlockSpec((tm, SLAB_W), row),
            pl.BlockSpec((tm, SLAB_W), row),
            pl.BlockSpec((tm, SLAB_W), row),
            pl.BlockSpec((tm, MIX_W), row),
            pl.BlockSpec((tm, MIX_W), row),
            pl.BlockSpec((1, HEADS * LANES, tm), lambda i: (i, 0, 0)),
        ),
        compiler_params=pltpu.CompilerParams(dimension_semantics=("parallel",), vmem_limit_bytes=VMEM_LIMIT),
        name="mixer_in_proj",
    )(x2, pre_g.reshape(1, d), wa, wb, wc, wd, cos_t, sin_t)


def _attn_kernel(q_ref, k_ref, vt_ref, lam_ref, g_ref, o_ref, qt_ref, m_ref, acc_ref, p_ref, s_ref, alpha_ref,
                 *, tq, tk, nb, kt_per_seq, lambda_init):
    i = pl.program_id(1)
    seqs = range(nb)

    @pl.when(i == 0)
    def _():
        qt_ref[...] = jnp.zeros_like(qt_ref)

    for b in seqs:
        q_t = q_ref[b].astype(F32).T
        for mp in range(2 * HEADS):
            dims = slice(mp * DIFF_DH, (mp + 1) * DIFF_DH)
            qt_ref[b, mp // 2, dims, (mp % 2) * tq:(mp % 2 + 1) * tq] = q_t[dims, :].astype(qt_ref.dtype)
    m_ref[...] = jnp.full_like(m_ref, NEG)
    acc_ref[...] = jnp.zeros_like(acc_ref)
    qpos = i * tq + (_iota((1, 2 * tq), 1) & (tq - 1))
    limit = ((qpos >> CHUNK_SHIFT) + 1) << CHUNK_SHIFT
    n_kt = (i * tq + tq + tk - 1) // tk

    def scores(b, h, j, rows=tk):
        kt = k_ref[b, pl.ds(pl.multiple_of(j * tk, tk), rows), :]
        s_ref[b, h, 0:rows, :] = jnp.dot(kt, qt_ref[b, h], preferred_element_type=F32)

    def softmax(b, h, j, rows, mask_from):
        def piece(r0):
            sp = s_ref[b, h, r0:r0 + ATT_RB, :]
            if mask_from is not None and r0 >= mask_from:
                key = j * tk + r0 + _iota((ATT_RB, 1), 0)
                sp = jnp.where(key < limit, sp, NEG)
            return sp

        mx = piece(0)
        for r0 in range(ATT_RB, rows, ATT_RB):
            mx = jnp.maximum(mx, piece(r0))
        m_old = m_ref[b, h]
        m_new = jnp.maximum(m_old, jnp.max(mx, axis=0, keepdims=True))
        for r0 in range(0, rows, ATT_RB):
            p_ref[b, h, r0:r0 + ATT_RB, :] = jnp.exp2(piece(r0) - m_new).astype(p_ref.dtype)
        alpha_ref[b, h] = jnp.exp2(m_old - m_new)
        m_ref[b, h] = m_new

    def values(b, h, j, rows):
        pv = jnp.dot(vt_ref[b * kt_per_seq + j, h * LANES:(h + 1) * LANES, 0:rows], p_ref[b, h, 0:rows, :],
                     preferred_element_type=F32)
        acc_ref[b, h] = alpha_ref[b, h] * acc_ref[b, h] + pv

    def step(j, rows, mask_from, prefetch_scores):
        for h in range(HEADS):
            for b in seqs:
                softmax(b, h, j, rows, mask_from)
                if h + 3 < HEADS:
                    scores(b, h + 3, j, rows)
                elif prefetch_scores:
                    scores(b, h + 3 - HEADS, j + 1)
                values(b, h, j, rows)

    def body(j, carry):
        step(j, tk, None, True)
        return carry

    for h in range(3):
        for b in seqs:
            scores(b, h, 0)
    lax.fori_loop(0, n_kt - 1, body, 0)
    last = n_kt - 1
    diagonal_first = (i * tq) == last * tk

    @pl.when(diagonal_first)
    def _():
        step(last, tq, 0, False)

    @pl.when(jnp.logical_not(diagonal_first))
    def _():
        step(last, tk, tq, False)

    lam_p = lam_ref[...]
    lam = (jnp.exp(jnp.sum(lam_p[0:1] * lam_p[1:2], axis=-1, keepdims=True))
           - jnp.exp(jnp.sum(lam_p[2:3] * lam_p[3:4], axis=-1, keepdims=True)) + lambda_init)
    for b in seqs:
        outs = []
        for h in range(HEADS):
            a = acc_ref[b, h]
            o = (a[:HEAD_W, :tq] / a[HEAD_W:HEAD_W + 1, :tq]
                 - lam * (a[:HEAD_W, tq:] / a[HEAD_W:HEAD_W + 1, tq:]))
            outs.append(o * lax.rsqrt(jnp.mean(o * o, axis=0, keepdims=True) + EPS))
        o_ref[b] = jnp.concatenate(outs, axis=0).T * g_ref[...] * (1.0 - lambda_init)


def _attn_call(dq, dk, dvt, lam_p, norm_g, lambda_init, bsz, seq, tq=ATT_TQ, tk=ATT_TK, nb=ATT_NB):
    assert tk == 2 * tq, "the last-key-tile split in _attn_kernel assumes tk == 2 tq"
    q3 = dq.reshape(bsz, seq, MIX_W)
    k3 = dk.reshape(bsz, seq, MIX_W)
    n_kt = seq // tk
    out = pl.pallas_call(
        functools.partial(_attn_kernel, tq=tq, tk=tk, nb=nb, kt_per_seq=n_kt, lambda_init=lambda_init),
        out_shape=jax.ShapeDtypeStruct((bsz, seq, MIX_W), F32),
        grid=(bsz // nb, seq // tq),
        in_specs=[
            pl.BlockSpec((nb, tq, MIX_W), lambda b, i: (b, i, 0)),
            pl.BlockSpec((nb, seq, MIX_W), lambda b, i: (b, 0, 0)),
            pl.BlockSpec((nb * n_kt, HEADS * LANES, tk), lambda b, i: (b, 0, 0)),
            pl.BlockSpec((4, DIFF_DH), lambda b, i: (0, 0)),
            pl.BlockSpec((1, MIX_W), lambda b, i: (0, 0)),
        ],
        out_specs=pl.BlockSpec((nb, tq, MIX_W), lambda b, i: (b, i, 0)),
        scratch_shapes=[
            pltpu.VMEM((nb, HEADS, MIX_W, 2 * tq), MXU_DTYPE),
            pltpu.VMEM((nb, HEADS, 1, 2 * tq), F32),
            pltpu.VMEM((nb, HEADS, LANES, 2 * tq), F32),
            pltpu.VMEM((nb, HEADS, tk, 2 * tq), MXU_DTYPE),
            pltpu.VMEM((nb, HEADS, tk, 2 * tq), F32),
            pltpu.VMEM((nb, HEADS, 1, 2 * tq), F32),
        ],
        compiler_params=pltpu.CompilerParams(dimension_semantics=("parallel", "arbitrary"),
                                             vmem_limit_bytes=VMEM_LIMIT),
        name="diff_attention",
    )(q3, k3, dvt, lam_p, jnp.tile(norm_g, HEADS).reshape(1, MIX_W))
    return out.reshape(bsz * seq, MIX_W)


def _head_block(x, h):
    blk = x[:, (h // 2) * LANES:(h // 2 + 1) * LANES]
    own = (_iota(blk.shape, 1) >> CHUNK_SHIFT) == (h % 2)
    return jnp.where(own, blk, jnp.zeros_like(blk))


def _bd_rows(blocks):
    zero = jnp.zeros_like(blocks[0])
    return jnp.concatenate([jnp.concatenate([b, zero] if h < 2 else [zero, b], axis=1)
                            for h, b in enumerate(blocks)], axis=0)


def _bd(x):
    return _bd_rows([_head_block(x, h) for h in range(HEADS)])


def _bd_keep(m):
    return _bd_rows([_head_block(m[h * CHUNK:(h + 1) * CHUNK, :], h) for h in range(HEADS)])


def _split(x):
    hi = x.astype(MXU_DTYPE)
    return hi, (x - hi.astype(F32)).astype(MXU_DTYPE)


def _bd_matmul3(a, b):
    ah, al = _split(a)
    bh, bl = _split(b)
    m = a.shape[0]
    r = jnp.dot(jnp.concatenate([ah, al], axis=0), _bd(bh), preferred_element_type=F32)
    return r[:m] + r[m:] + jnp.dot(ah, _bd(bl), preferred_element_type=F32)


def _chunk_slices(t):
    return [slice(c * CHUNK, (c + 1) * CHUNK) for c in range(t // CHUNK)]


def _ssd_steps(slab_ref, cw_ref, cb_ref, alog_ref, dtb_ref, dskip_ref, g_ref, y_ref,
               carry_ref, buf_ref, st_ref, xc_ref, xcd_ref, bm_ref, cm_ref, lm_ref, ea_ref, el_ref, yo_ref, t, b):
    ltile, _, pos, mm = _chunk_consts(t)
    expand = _head_expand(0)
    xbc = _silu(_causal_conv(slab_ref[b,:, MIX_W:4 * MIX_W], cw_ref, carry_ref, buf_ref, t) + cb_ref[...])
    x = xbc[:, 0:MIX_W]
    bm_ref[...] = xbc[:, MIX_W:2 * MIX_W].astype(bm_ref.dtype)
    cm_ref[...] = xbc[:, 2 * MIX_W:3 * MIX_W].astype(cm_ref.dtype)
    yield
    dt = _softplus(slab_ref[b,:, 4 * MIX_W:] + dtb_ref[...])
    da = dt * (-jnp.exp(alog_ref[...]))
    dt_e = _dot_xsel(dt, expand)
    da_e = _dot_xsel(da, expand)
    yield
    acs_e = _dot_sel(ltile, da_e)
    acs_last_e = _chunk_last(acs_e)
    ldiff = _dot_sel(ltile, jnp.where(pos > mm, da_e, 0.0))
    lm_ref[...] = jnp.where(mm <= pos, jnp.exp(ldiff), 0.0)
    xc = x * dt_e
    xc_ref[...] = xc.astype(xc_ref.dtype)
    xcd_ref[...] = (xc * jnp.exp(acs_last_e - acs_e)).astype(xcd_ref.dtype)
    ea_ref[...] = jnp.exp(acs_e)
    el_ref[...] = jnp.exp(acs_last_e)
    yield
    for c in range(t // CHUNK):
        r0 = c * CHUNK
        rows = slice(r0, r0 + CHUNK)
        for grp in range(2):
            gl = slice(grp * SSD_N, (grp + 1) * SSD_N)
            bg = bm_ref[rows, gl]
            cg = cm_ref[rows, gl]
            cb = _dot_nt(cg, bg)
            ys = []
            for hh in range(2):
                hl = slice(grp * SSD_N + hh * HEAD_W, grp * SSD_N + (hh + 1) * HEAD_W)
                ys.append(_dot(cb * lm_ref[rows, hl], xc_ref[rows, hl]))
            y_diag = jnp.concatenate(ys, axis=-1)
            st = st_ref[grp]
            y_off = _dot(cg, st) * ea_ref[rows, gl]
            st_ref[grp] = st * el_ref[r0:r0 + 1, gl] + _dot_tn(bg, xcd_ref[rows, gl])
            yo_ref[rows, gl] = y_diag + y_off
        yield
    y = (yo_ref[...] + x * dskip_ref[...]) * _silu(slab_ref[b,:, 0:MIX_W])
    gs = g_ref[...]
    for grp in range(2):
        gl = slice(grp * SSD_N, (grp + 1) * SSD_N)
        y_ref[b, :, gl] = _rms(y[:, gl], gs[:, gl])


def _gdn_steps(slab_ref, cw_ref, alog_ref, dtb_ref, g_ref, y_ref,
               carry_ref, buf_ref, st_ref, qn_ref, kn_ref, qd_ref, kd_ref, kb_ref, vb_ref, kbg_ref, dec_ref,
               egl_ref, u_ref, w_ref, attn_ref, o_ref, t, b):
    ltile, _, pos, mm = _chunk_consts(t)
    ones_h = _head_ones()
    qkv = _silu(_causal_conv(slab_ref[b,:, 0:3 * MIX_W], cw_ref, carry_ref, buf_ref, t))
    q = qkv[:, 0:MIX_W]
    k = qkv[:, MIX_W:2 * MIX_W]
    v = qkv[:, 2 * MIX_W:]
    yield
    qn = q * lax.rsqrt(_dot_xsel(q * q, ones_h, terms=2) + EPS) * (HEAD_W ** -0.5)
    kn = k * lax.rsqrt(_dot_xsel(k * k, ones_h, terms=2) + EPS)
    yield
    ba = slab_ref[b,:, 4 * MIX_W:]
    beta_e = _dot_xsel(_sigmoid(ba), _head_expand(0))
    g4 = -jnp.exp(alog_ref[...]) * _softplus(ba + dtb_ref[...])
    g_e = _dot_xsel(g4, _head_expand(HEADS))
    yield
    gc_e = _dot_sel(ltile, g_e)
    gl_e = _chunk_last(gc_e)
    diff = _dot_sel(ltile, jnp.where(pos > mm, g_e, 0.0))
    dec_ref[...] = jnp.where(mm <= pos, jnp.exp(diff), 0.0)
    egc = jnp.exp(gc_e)
    kb = kn * beta_e
    qn_ref[...] = qn
    kn_ref[...] = kn
    qd_ref[...] = qn * egc
    kd_ref[...] = kn * jnp.exp(gl_e - gc_e)
    kb_ref[...] = kb
    vb_ref[...] = v * beta_e
    kbg_ref[...] = kb * egc
    egl_ref[...] = jnp.exp(gl_e)
    yield

    r_ss = _iota((CHUNK, MIX_W), 0)
    m_ss = _iota((CHUNK, MIX_W), 1) & (CHUNK - 1)
    eye_ss = jnp.where(r_ss == m_ss, 1.0, 0.0).astype(F32)
    chunk_rows = _chunk_slices(t)

    ys, tms = [], []
    for rows in chunk_rows:
        lhs = jnp.concatenate([kb_ref[rows, :], qn_ref[rows, :]], axis=0).astype(MXU_DTYPE)
        raw = lax.dot_general(lhs, _bd(kn_ref[rows, :].astype(MXU_DTYPE)), (((1,), (1,)), ((), ())),
                              preferred_element_type=F32)
        dec = dec_ref[rows, :]
        attn_ref[rows, :] = raw[CHUNK:] * dec
        y = jnp.where(m_ss < r_ss, -(raw[:CHUNK] * dec), 0.0)
        ys.append(y)
        tms.append(eye_ss + y)
    yield
    ys = [_bd_matmul3(y, y) for y in ys]
    yield
    for _ in range(4):
        rs = [_bd_matmul3(jnp.concatenate([tm, y], axis=0), y) for tm, y in zip(tms, ys)]
        tms = [tm + r[:CHUNK] for tm, r in zip(tms, rs)]
        ys = [r[CHUNK:] for r in rs]
        yield
    tms = [tm + _bd_matmul3(tm, y) for tm, y in zip(tms, ys)]
    yield
    for rows, tm in zip(chunk_rows, tms):
        rhs = jnp.concatenate([_bd(vb_ref[rows, :].astype(MXU_DTYPE)),
                               _bd(kbg_ref[rows, :].astype(MXU_DTYPE))], axis=1)
        uw = jnp.dot(tm.astype(MXU_DTYPE), rhs, preferred_element_type=F32)
        u_ref[rows, :] = uw[:, :MIX_W]
        w_ref[rows, :] = uw[:, MIX_W:]
    yield

    st = st_ref[...]
    for c, rows in enumerate(chunk_rows):
        lhs = jnp.concatenate([w_ref[rows, :], qd_ref[rows, :]], axis=0).astype(MXU_DTYPE)
        r = jnp.dot(lhs, st.astype(MXU_DTYPE), preferred_element_type=F32)
        v_new = u_ref[rows, :] - r[:CHUNK]
        v_new_c = v_new.astype(MXU_DTYPE)
        o_ref[rows, :] = r[CHUNK:] + jnp.dot(attn_ref[rows, :].astype(MXU_DTYPE), _bd(v_new_c),
                                             preferred_element_type=F32)
        kv = lax.dot_general(kd_ref[rows, :].astype(MXU_DTYPE), v_new_c, (((0,), (0,)), ((), ())),
                             preferred_element_type=F32)
        st = st * egl_ref[c * CHUNK:c * CHUNK + 1, :] + _bd_keep(kv)
        yield
    st_ref[...] = st

    o = o_ref[...]
    o = o * lax.rsqrt(_dot_xsel(o * o, ones_h, terms=2) * (1.0 / HEAD_W) + EPS) * g_ref[...]
    y_ref[b] =o * _silu(slab_ref[b,:, 3 * MIX_W:4 * MIX_W])


def _mlstm_steps(slab_ref, bias_ref, g_ref, y_ref, st_ref, m_ref, dl_ref, b_ref, ge_ref, dmax_ref, qk_ref,
                 o_ref, t, b):
    ltile, csame, pos, mm = _chunk_consts(t)
    gates = slab_ref[b,:, 4 * MIX_W:] + bias_ref[...]
    li_e = _dot_xsel(gates, _head_expand(0))
    lf_e = _dot_xsel(-_softplus(-gates), _head_expand(HEADS))
    yield
    b_e = _dot_sel(ltile, lf_e)
    dlog = _dot_sel(ltile, jnp.where(pos > mm, lf_e, 0.0)) + _dot_sel(csame, jnp.where(pos == mm, li_e, 0.0))
    dl = jnp.where(mm <= pos, dlog, NEG)
    yield
    lane_head = _iota((t, MIX_W), 1) >> CHUNK_SHIFT
    dmax = jnp.full((t, MIX_W), NEG, F32)
    for h in range(HEADS):
        in_head = lane_head == h
        dmax = jnp.where(in_head, jnp.max(jnp.where(in_head, dl, NEG), axis=-1, keepdims=True), dmax)
    dl_ref[...] = dl
    dmax_ref[...] = dmax
    b_ref[...] = b_e
    ge_ref[...] = _chunk_last(b_e) - b_e + li_e
    yield

    ones_bd = _head_ones().astype(MXU_DTYPE)
    ones_rows = jnp.ones((CHUNK, MIX_W), MXU_DTYPE)
    chunk_rows = _chunk_slices(t)
    k_scale = HEAD_W ** -0.5
    for rows in chunk_rows:
        k_c = (slab_ref[b,rows, MIX_W:2 * MIX_W] * k_scale).astype(MXU_DTYPE)
        qk_ref[rows, :] = lax.dot_general(slab_ref[b,rows, 0:MIX_W].astype(MXU_DTYPE), _bd(k_c),
                                          (((1,), (1,)), ((), ())), preferred_element_type=F32)
    yield

    st = st_ref[...]
    m_st = m_ref[...]
    for rows in chunk_rows:
        q_c = slab_ref[b,rows, 0:MIX_W].astype(MXU_DTYPE)
        k_c = slab_ref[b,rows, MIX_W:2 * MIX_W] * k_scale
        v_c = slab_ref[b,rows, 2 * MIX_W:3 * MIX_W].astype(MXU_DTYPE)
        b_c = b_ref[rows, :]
        m_t = jnp.maximum(b_c + m_st, dmax_ref[rows, :])
        w_st = jnp.exp(b_c + m_st - m_t)
        s = qk_ref[rows, :] * jnp.exp(dl_ref[rows, :] - m_t)
        qs = jnp.dot(q_c, st.astype(MXU_DTYPE), preferred_element_type=F32)
        sv = jnp.dot(s.astype(MXU_DTYPE), jnp.concatenate([_bd(v_c), ones_bd], axis=1),
                     preferred_element_type=F32)
        num = w_st * qs[:, :MIX_W] + sv[:, :MIX_W]
        den = w_st * qs[:, MIX_W:] + sv[:, MIX_W:]
        o_ref[rows, :] = num / jnp.maximum(jnp.abs(den), jnp.exp(-m_t))
        ge_c = ge_ref[rows, :]
        b_last = b_c[CHUNK - 1:CHUNK, :]
        m_new = jnp.maximum(b_last + m_st, jnp.max(ge_c, axis=0, keepdims=True))
        w_old = jnp.exp(b_last + m_st - m_new)
        k_w = (k_c * jnp.exp(ge_c - m_new)).astype(MXU_DTYPE)
        upd = lax.dot_general(k_w, jnp.concatenate([v_c, ones_rows], axis=1), (((0,), (0,)), ((), ())),
                              preferred_element_type=F32)
        st = (st * jnp.concatenate([w_old, w_old], axis=1)
              + jnp.concatenate([_bd_keep(upd[:, :MIX_W]), _bd_keep(upd[:, MIX_W:])], axis=1))
        m_st = m_new
        yield
    st_ref[...] = st
    m_ref[...] = m_st

    hcat = o_ref[...] * _sigmoid(slab_ref[b,:, 3 * MIX_W:4 * MIX_W])
    ms = _dot_xsel(hcat * hcat, _head_ones(), terms=2) * (1.0 / HEAD_W)
    y_ref[b] =hcat * lax.rsqrt(ms + EPS) * g_ref[...]


_N_GDN_REFS = (5, 1, 16)
_N_SSD_REFS = (7, 1, 11)
_N_MLSTM_REFS = (3, 1, 8)


def _mixers_kernel(*refs, t, nb):
    counts = (_N_GDN_REFS, _N_SSD_REFS, _N_MLSTM_REFS)
    pos = 0
    ins, outs = [], []
    for c in counts:
        ins.append(refs[pos:pos + c[0]])
        pos += c[0]
    for c in counts:
        outs.append(refs[pos:pos + c[1]])
        pos += c[1]
    scratch = []
    for b in range(nb):
        per_mixer = []
        for c in counts:
            per_mixer.append(refs[pos:pos + c[2]])
            pos += c[2]
        scratch.append(per_mixer)

    @pl.when(pl.program_id(1) == 0)
    def _():
        for b in range(nb):
            gdn_s, ssd_s, ml_s = scratch[b]
            for state_ref in (gdn_s[0], gdn_s[2], ssd_s[0], ssd_s[2], ml_s[0], ml_s[1]):
                state_ref[...] = jnp.zeros_like(state_ref)

    gdn = [_gdn_steps(*ins[0], *outs[0], *scratch[b][0], t, b) for b in range(nb)]
    ssd = [_ssd_steps(*ins[1], *outs[1], *scratch[b][1], t, b) for b in range(nb)]
    mls = [_mlstm_steps(*ins[2], *outs[2], *scratch[b][2], t, b) for b in range(nb)]
    streams = gdn + gdn + ssd + mls
    while streams:
        for stream in list(streams):
            if stream in streams and next(stream, StopIteration) is StopIteration:
                streams = [s for s in streams if s is not stream]


def _mixers_call(gdn_slab, ssd_slab, ml_slab, p, bsz, seq, t=MIX_T, nb=MIX_NB):
    ch = 3 * MIX_W
    lane_pad = lambda v, before: jnp.pad(v, (before, LANES - before - v.shape[0])).reshape(1, LANES)
    full = lambda b, i: (0, 0)
    slab_spec = pl.BlockSpec((nb, t, SLAB_W), lambda b, i: (b, i, 0))
    vec = lambda n: pl.BlockSpec((1, n), full)
    tile = lambda dtype=F32: pltpu.VMEM((t, MIX_W), dtype)
    out_spec = pl.BlockSpec((nb, t, MIX_W), lambda b, i: (b, i, 0))
    out_shape = jax.ShapeDtypeStruct((bsz, seq, MIX_W), F32)
    conv_scratch = lambda: [pltpu.VMEM((8, ch), F32), pltpu.VMEM((t + 8, ch), F32)]
    gdn_in = [gdn_slab.reshape(bsz, seq, SLAB_W), p["gdn_conv_w"], lane_pad(p["gdn_a_log"], HEADS),
              lane_pad(p["gdn_dt_bias"], HEADS), jnp.tile(p["gdn_norm_g"], HEADS).reshape(1, MIX_W)]
    gdn_specs = [slab_spec, pl.BlockSpec((CONV_K, ch), full), vec(LANES), vec(LANES), vec(MIX_W)]
    gdn_scratch = lambda: conv_scratch() + [pltpu.VMEM((MIX_W, MIX_W), F32)] + [tile() for _ in range(13)]
    ssd_in = [ssd_slab.reshape(bsz, seq, SLAB_W), p["ssd_conv_w"], p["ssd_conv_b"].reshape(1, ch),
              lane_pad(p["ssd_a_log"], 0), lane_pad(p["ssd_dt_bias"], 0),
              jnp.repeat(p["ssd_d"], HEAD_W).reshape(1, MIX_W), p["ssd_norm_g"].reshape(1, MIX_W)]
    ssd_specs = [slab_spec, pl.BlockSpec((CONV_K, ch), full), vec(ch), vec(LANES), vec(LANES), vec(MIX_W), vec(MIX_W)]
    ssd_scratch = lambda: (conv_scratch() + [pltpu.VMEM((2, SSD_N, SSD_N), F32)]
                           + [tile(MXU_DTYPE) for _ in range(4)] + [tile() for _ in range(4)])
    ml_in = [ml_slab.reshape(bsz, seq, SLAB_W),
             lane_pad(jnp.concatenate([p["mlstm_i_bias"], p["mlstm_f_bias"]]), 0), p["mlstm_norm_g"].reshape(1, MIX_W)]
    ml_specs = [slab_spec, vec(LANES), vec(MIX_W)]
    ml_scratch = lambda: ([pltpu.VMEM((MIX_W, 2 * MIX_W), F32), pltpu.VMEM((1, MIX_W), F32)]
                          + [tile() for _ in range(6)])
    assert (len(gdn_in), 1, len(gdn_scratch())) == _N_GDN_REFS
    assert (len(ssd_in), 1, len(ssd_scratch())) == _N_SSD_REFS
    assert (len(ml_in), 1, len(ml_scratch())) == _N_MLSTM_REFS
    scratch = []
    for _ in range(nb):
        scratch += gdn_scratch() + ssd_scratch() + ml_scratch()
    outs = pl.pallas_call(
        functools.partial(_mixers_kernel, t=t, nb=nb),
        out_shape=(out_shape, out_shape, out_shape),
        grid=(bsz // nb, seq // t),
        in_specs=gdn_specs + ssd_specs + ml_specs,
        out_specs=(out_spec, out_spec, out_spec),
        scratch_shapes=scratch,
        compiler_params=pltpu.CompilerParams(dimension_semantics=("parallel", "arbitrary"),
                                             vmem_limit_bytes=VMEM_LIMIT),
        name="recurrent_mixers",
    )(*gdn_in, *ssd_in, *ml_in)
    return [o.reshape(bsz * seq, MIX_W) for o in outs]


def _rope_tables(seq):
    inv_freq = ROPE_THETA ** (-jnp.arange(0, DIFF_DH, 2, dtype=F32) / DIFF_DH)
    ang = jnp.arange(seq, dtype=F32)[:, None] * inv_freq[None, :]
    cos, sin = jnp.cos(ang), jnp.sin(ang)
    reps = LANES // DIFF_DH
    cos_t = jnp.tile(jnp.concatenate([cos, cos], axis=-1), (1, reps))
    sin_t = jnp.tile(jnp.concatenate([-sin, sin], axis=-1), (1, reps))
    return cos_t, sin_t


def kernel(x, ffn1_pre_g, ffn1_w_gate, ffn1_w_up, ffn1_w_down, ffn1_post_g, mix_pre_g, w_in, gdn_conv_w, gdn_a_log, gdn_dt_bias, gdn_norm_g, diff_lam_q1, diff_lam_k1, diff_lam_q2, diff_lam_k2, diff_norm_g, ssd_conv_w, ssd_conv_b, ssd_a_log, ssd_dt_bias, ssd_d, ssd_norm_g, mlstm_i_bias, mlstm_f_bias, mlstm_norm_g, w_out, mix_post_g, ffn2_pre_g, ffn2_w_gate, ffn2_w_up, ffn2_w_down, ffn2_post_g):
    bsz, seq, d = x.shape
    x2 = x.reshape(bsz * seq, d)
    cos_t, sin_t = _rope_tables(seq)
    for l in range(DEPTH):
        lambda_init = 0.8 - 0.6 * math.exp(-0.3 * l)
        x2 = _ffn_call(x2, ffn1_pre_g[l], ffn1_w_gate[l], ffn1_w_up[l], ffn1_w_down[l], ffn1_post_g[l])
        gdn_slab, ssd_slab, ml_slab, dq, dk, dv = _proj_call(x2, mix_pre_g[l], w_in[l], cos_t, sin_t, seq)
        mixer_params = dict(
            gdn_conv_w=gdn_conv_w[l], gdn_a_log=gdn_a_log[l], gdn_dt_bias=gdn_dt_bias[l], gdn_norm_g=gdn_norm_g[l],
            ssd_conv_w=ssd_conv_w[l], ssd_conv_b=ssd_conv_b[l], ssd_a_log=ssd_a_log[l], ssd_dt_bias=ssd_dt_bias[l],
            ssd_d=ssd_d[l], ssd_norm_g=ssd_norm_g[l],
            mlstm_i_bias=mlstm_i_bias[l], mlstm_f_bias=mlstm_f_bias[l], mlstm_norm_g=mlstm_norm_g[l])
        out_a, out_c, out_d = _mixers_call(gdn_slab, ssd_slab, ml_slab, mixer_params, bsz, seq)
        lam_p = jnp.stack([diff_lam_q1[l], diff_lam_k1[l], diff_lam_q2[l], diff_lam_k2[l]])
        out_b = _attn_call(dq, dk, dv, lam_p, diff_norm_g[l], lambda_init, bsz, seq)
        x2 = _ffn_call(x2, ffn2_pre_g[l], ffn2_w_gate[l], ffn2_w_up[l], ffn2_w_down[l], ffn2_post_g[l],
                       mix=((out_a, out_b, out_c, out_d), w_out[l], mix_post_g[l]))
    return x2.reshape(bsz, seq, d)
```
